```python
import math, functools
import jax, jax.numpy as jnp
from jax import lax
import numpy as np

D_MODEL = 1024
BATCH = 8
SEQ = 2048
DEPTH = 2
DEC_BATCH = 32
DEC_SEQ = 8
PAST_LEN = 8192
PAGE_SIZE = 128

N_META = 16
DA_HEADS = 4
DA_HEAD_DIM = 64
DA_QK_WIDTH = DA_HEADS * 2 * DA_HEAD_DIM
DA_V_WIDTH = DA_HEADS * 2 * DA_HEAD_DIM
Q_BLOCK = 128
GDN_HEADS = 4
GDN_DK = 128
GDN_DV = 128
GDN_K_WIDTH = GDN_HEADS * GDN_DK
GDN_V_WIDTH = GDN_HEADS * GDN_DV
GDN_QKV_WIDTH = 2 * GDN_K_WIDTH + GDN_V_WIDTH
GDN_CONV = 4
GDN_CHUNK = 64
D_FF = 2816
FFN_CONV = 3
RMS_EPS = 1e-6
IN_WIDTHS = (DA_QK_WIDTH, DA_QK_WIDTH, DA_V_WIDTH, GDN_QKV_WIDTH, GDN_V_WIDTH, GDN_HEADS, GDN_HEADS, D_MODEL, D_MODEL)
D_IN = sum(IN_WIDTHS)
SPLIT_POINTS = tuple(int(s) for s in np.cumsum(IN_WIDTHS)[:-1])

kernel_name = 'hybrid_diffattn_gdn_convffn_step'


def rmsnorm(x, w):
    xf = x.astype(jnp.float32)
    y = xf * lax.rsqrt(jnp.mean(xf * xf, axis=-1, keepdims=True) + RMS_EPS)
    return (y * w.astype(jnp.float32)).astype(x.dtype)


def l2norm(x):
    xf = x.astype(jnp.float32)
    return xf * lax.rsqrt(jnp.sum(xf * xf, axis=-1, keepdims=True) + 1e-6)


def causal_dwconv(x, buf, w, b=None):
    width = w.shape[0]
    t = x.shape[1]
    xp = jnp.concatenate([buf.astype(x.dtype), x], axis=1)
    y = sum(xp[:, j:j + t] * w[j] for j in range(width))
    if b is not None:
        y = y + b
    return y, xp[:, xp.shape[1] - (width - 1):]


def diff_attn_prompt(q, k, v, lam):
    bsz, t, h, _, d = q.shape
    n_blk = -(-t // Q_BLOCK)
    pad = n_blk * Q_BLOCK - t
    qb = jnp.pad(q * (d ** -0.5), ((0, 0), (0, pad), (0, 0), (0, 0), (0, 0)))
    qb = jnp.moveaxis(qb.reshape(bsz, n_blk, Q_BLOCK, h, 2, d), 1, 0)
    vf = v.astype(jnp.float32)
    k_pos = jnp.arange(t)

    def one_block(args):
        q_blk, start = args
        q_pos = start + jnp.arange(Q_BLOCK)
        s = jnp.einsum('bqhmd,bkhmd->bhmqk', q_blk, k, preferred_element_type=jnp.float32)
        s = jnp.where(k_pos[None, :] <= q_pos[:, None], s, -jnp.inf)
        pr = jax.nn.softmax(s, axis=-1)
        a = pr[:, :, 0] - lam * pr[:, :, 1]
        return jnp.einsum('bhqk,bkhe->bqhe', a, vf)

    out = lax.map(one_block, (qb, jnp.arange(n_blk) * Q_BLOCK))
    out = jnp.moveaxis(out, 0, 1).reshape(bsz, n_blk * Q_BLOCK, h, 2 * d)[:, :t]
    return out.astype(q.dtype)


def diff_attn_sample(q, k, v, lam, k_past, v_past):
    t = q.shape[1]
    d = q.shape[-1]
    n_past = k_past.shape[1]
    qs = q * (d ** -0.5)
    s_past = jnp.einsum('bqhmd,bkhmd->bhmqk', qs, k_past, preferred_element_type=jnp.float32)
    s_new = jnp.einsum('bqhmd,bkhmd->bhmqk', qs, k, preferred_element_type=jnp.float32)
    s_new = jnp.where(jnp.tril(jnp.ones((t, t), bool)), s_new, -jnp.inf)
    pr = jax.nn.softmax(jnp.concatenate([s_past, s_new], axis=-1), axis=-1)
    a = pr[:, :, 0] - lam * pr[:, :, 1]
    out = (jnp.einsum('bhqk,bkhe->bqhe', a[..., :n_past], v_past.astype(jnp.float32))
           + jnp.einsum('bhqk,bkhe->bqhe', a[..., n_past:], v.astype(jnp.float32)))
    return out.astype(q.dtype)


def gated_delta_chunked(q, k, v, g, beta, state, chunk):
    bsz, t, h, dk = q.shape
    dv = v.shape[-1]
    n = t // chunk

    def blocks(a):
        a = a.reshape((bsz, n, chunk) + a.shape[2:])
        return jnp.moveaxis(jnp.moveaxis(a, 2, 3), 1, 0)

    qc, kc, vc, gc, bc = blocks(q), blocks(k), blocks(v), blocks(g), blocks(beta)
    gcum = jnp.cumsum(gc, axis=-1)
    causal = jnp.tril(jnp.ones((chunk, chunk), bool))
    strict = jnp.tril(jnp.ones((chunk, chunk), bool), -1)
    decay = jnp.exp(jnp.where(causal, gcum[..., :, None] - gcum[..., None, :], -jnp.inf))
    kb = kc * bc[..., None]
    lower = jnp.where(strict, jnp.einsum('nbhcd,nbhed->nbhce', kb, kc) * decay, 0.0)
    a_mat = lower + jnp.eye(chunk, dtype=jnp.float32)
    rhs = jnp.concatenate([vc * bc[..., None], kb * jnp.exp(gcum)[..., None]], axis=-1)
    sol = lax.linalg.triangular_solve(a_mat, rhs, left_side=True, lower=True, unit_diagonal=True)
    u, w = sol[..., :dv], sol[..., dv:]

    def step(s, xs):
        q_i, k_i, u_i, w_i, g_i, dec_i = xs
        v_new = u_i - jnp.einsum('bhcd,bhde->bhce', w_i, s)
        intra = jnp.einsum('bhcd,bhed->bhce', q_i, k_i) * dec_i
        o = (jnp.einsum('bhcd,bhde->bhce', q_i * jnp.exp(g_i)[..., None], s)
             + jnp.einsum('bhce,bhef->bhcf', intra, v_new))
        g_last = g_i[..., -1:]
        s = (s * jnp.exp(g_last)[..., None]
             + jnp.einsum('bhcd,bhce->bhde', k_i * jnp.exp(g_last - g_i)[..., None], v_new))
        return s, o

    state, o = lax.scan(step, state, (qc, kc, u, w, gcum, decay))
    o = jnp.moveaxis(jnp.moveaxis(o, 0, 1), 2, 3).reshape(bsz, t, h, dv)
    return o, state


def decoder_layer(x, p, lam_init, attn_fn, segments, gdn_conv_buf, gdn_state, ffn_conv_buf):
    f32 = jnp.float32
    bsz, t, _ = x.shape
    xn = rmsnorm(x, p['ln_mix_pre'])
    h = xn @ p['w_in']
    da_q, da_k, da_v, gdn_qkv, gdn_z, gdn_b, gdn_a, gate_a, gate_b = jnp.split(h, SPLIT_POINTS, axis=-1)

    q = da_q.reshape(bsz, t, DA_HEADS, 2, DA_HEAD_DIM)
    k = da_k.reshape(bsz, t, DA_HEADS, 2, DA_HEAD_DIM)
    v = da_v.reshape(bsz, t, DA_HEADS, 2 * DA_HEAD_DIM)
    lam = (jnp.exp(jnp.sum(p['lq1'].astype(f32) * p['lk1'].astype(f32)))
           - jnp.exp(jnp.sum(p['lq2'].astype(f32) * p['lk2'].astype(f32))) + lam_init)
    o_a = attn_fn(q, k, v, lam)
    o_a = (rmsnorm(o_a, p['da_subln']) * (1.0 - lam_init)).reshape(bsz, t, DA_V_WIDTH)

    qkv, new_gdn_conv = causal_dwconv(gdn_qkv, gdn_conv_buf, p['gdn_conv_w'])
    qkv = jax.nn.silu(qkv)
    gq, gk, gv = jnp.split(qkv, (GDN_K_WIDTH, 2 * GDN_K_WIDTH), axis=-1)
    gq = l2norm(gq.reshape(bsz, t, GDN_HEADS, GDN_DK)) * (GDN_DK ** -0.5)
    gk = l2norm(gk.reshape(bsz, t, GDN_HEADS, GDN_DK))
    gv = gv.reshape(bsz, t, GDN_HEADS, GDN_DV).astype(f32)
    beta = jax.nn.sigmoid(gdn_b.astype(f32))
    g = -jnp.exp(p['gdn_a_log'].astype(f32)) * jax.nn.softplus(gdn_a.astype(f32) + p['gdn_dt_bias'].astype(f32))
    s = gdn_state.astype(f32)
    outs = []
    start = 0
    for length, chunk in segments:
        sl = slice(start, start + length)
        o_seg, s = gated_delta_chunked(gq[:, sl], gk[:, sl], gv[:, sl], g[:, sl], beta[:, sl], s, chunk)
        outs.append(o_seg)
        start += length
    o_b = jnp.concatenate(outs, axis=1)
    o_b = rmsnorm(o_b, p['gdn_norm']) * jax.nn.silu(gdn_z.reshape(bsz, t, GDN_HEADS, GDN_DV).astype(f32))
    o_b = o_b.reshape(bsz, t, GDN_V_WIDTH).astype(x.dtype)

    mix = jax.nn.sigmoid(gate_a) * (o_a @ p['w_branch_a']) + jax.nn.sigmoid(gate_b) * (o_b @ p['w_branch_b'])
    x = x + rmsnorm(mix @ p['w_out'], p['ln_mix_post'])

    xn = rmsnorm(x, p['ln_ffn_pre'])
    up, gate = jnp.split(xn @ p['w_ffn_in'], 2, axis=-1)
    gate, new_ffn_conv = causal_dwconv(gate, ffn_conv_buf, p['ffn_conv_w'], p['ffn_conv_b'])
    x = x + rmsnorm((jax.nn.gelu(gate) * up) @ p['w_down'], p['ln_ffn_post'])
    return x, k, v, new_gdn_conv, s.astype(x.dtype), new_ffn_conv


def setup_inputs(seed: int = 0) -> dict:
    key = jax.random.key(seed)
    ks = iter(jax.random.split(key, 40))

    def nrm(shape, scale=1.0):
        return jax.random.normal(next(ks), shape, jnp.float32) * scale

    n_pages = PAST_LEN // PAGE_SIZE
    n_used = DEC_BATCH * n_pages
    n_pool = n_used + max(1, n_used // 4)
    page_table = jax.random.permutation(next(ks), n_pool)[:n_used].reshape(DEC_BATCH, n_pages).astype(jnp.int32)
    a_log = jnp.log(jax.random.uniform(next(ks), (DEPTH, GDN_HEADS), jnp.float32, 1.0, 16.0))
    dt = jnp.exp(jax.random.uniform(next(ks), (DEPTH, GDN_HEADS), jnp.float32, math.log(1e-3), math.log(1e-1)))
    dt_bias = dt + jnp.log(-jnp.expm1(-dt))
    return {
        'x_prompt': nrm((BATCH, SEQ, D_MODEL)),
        'x_sample': nrm((DEC_BATCH, DEC_SEQ, D_MODEL)),
        'cache_k': nrm((DEPTH, n_pool, PAGE_SIZE, DA_HEADS, 2, DA_HEAD_DIM)),
        'cache_v': nrm((DEPTH, n_pool, PAGE_SIZE, DA_HEADS, 2 * DA_HEAD_DIM)),
        'page_table': page_table,
        'state_gdn': nrm((DEPTH, DEC_BATCH, GDN_HEADS, GDN_DK, GDN_DV), 0.1),
        'state_gdn_conv': nrm((DEPTH, DEC_BATCH, GDN_CONV - 1, GDN_QKV_WIDTH)),
        'state_ffn_conv': nrm((DEPTH, DEC_BATCH, FFN_CONV - 1, D_FF)),
        'meta_tokens': nrm((N_META, D_MODEL)),
        'ln_mix_pre': 1.0 + nrm((DEPTH, D_MODEL), 0.02),
        'ln_mix_post': 1.0 + nrm((DEPTH, D_MODEL), 0.02),
        'ln_ffn_pre': 1.0 + nrm((DEPTH, D_MODEL), 0.02),
        'ln_ffn_post': 1.0 + nrm((DEPTH, D_MODEL), 0.02),
        'w_in': nrm((DEPTH, D_MODEL, D_IN), D_MODEL ** -0.5),
        'da_lambda_q1': nrm((DEPTH, DA_HEAD_DIM), 0.1),
        'da_lambda_k1': nrm((DEPTH, DA_HEAD_DIM), 0.1),
        'da_lambda_q2': nrm((DEPTH, DA_HEAD_DIM), 0.1),
        'da_lambda_k2': nrm((DEPTH, DA_HEAD_DIM), 0.1),
        'da_subln': 1.0 + nrm((DEPTH, 2 * DA_HEAD_DIM), 0.02),
        'gdn_conv_w': nrm((DEPTH, GDN_CONV, GDN_QKV_WIDTH), GDN_CONV ** -0.5),
        'gdn_a_log': a_log,
        'gdn_dt_bias': dt_bias,
        'gdn_norm': 1.0 + nrm((DEPTH, GDN_DV), 0.02),
        'w_branch_a': nrm((DEPTH, DA_V_WIDTH, D_MODEL), DA_V_WIDTH ** -0.5),
        'w_branch_b': nrm((DEPTH, GDN_V_WIDTH, D_MODEL), GDN_V_WIDTH ** -0.5),
        'w_out': nrm((DEPTH, D_MODEL, D_MODEL), D_MODEL ** -0.5),
        'w_ffn_in': nrm((DEPTH, D_MODEL, 2 * D_FF), D_MODEL ** -0.5),
        'ffn_conv_w': nrm((DEPTH, FFN_CONV, D_FF), FFN_CONV ** -0.5),
        'ffn_conv_b': nrm((DEPTH, D_FF), 0.01),
        'w_down': nrm((DEPTH, D_FF, D_MODEL), D_FF ** -0.5),
    }


def reference(x_prompt, x_sample, cache_k, cache_v, page_table, state_gdn, state_gdn_conv, state_ffn_conv,
              meta_tokens, ln_mix_pre, ln_mix_post, ln_ffn_pre, ln_ffn_post, w_in,
              da_lambda_q1, da_lambda_k1, da_lambda_q2, da_lambda_k2, da_subln,
              gdn_conv_w, gdn_a_log, gdn_dt_bias, gdn_norm, w_branch_a, w_branch_b, w_out,
              w_ffn_in, ffn_conv_w, ffn_conv_b, w_down):
    bp, seq, _ = x_prompt.shape
    ds, dseq, _ = x_sample.shape
    dt = x_prompt.dtype
    meta = jnp.broadcast_to(meta_tokens[None].astype(dt), (bp, N_META, D_MODEL))
    xp = jnp.concatenate([meta, x_prompt], axis=1)
    xs = x_sample
    prompt_segments = ((N_META, N_META), (seq, GDN_CHUNK))
    sample_segments = ((dseq, math.gcd(dseq, GDN_CHUNK)),)
    zero_gdn_conv = jnp.zeros((bp, GDN_CONV - 1, GDN_QKV_WIDTH), dt)
    zero_gdn = jnp.zeros((bp, GDN_HEADS, GDN_DK, GDN_DV), jnp.float32)
    zero_ffn_conv = jnp.zeros((bp, FFN_CONV - 1, D_FF), dt)
    kp_l, vp_l, ks_l, vs_l, sp_l, ss_l, cp_l, cs_l, fp_l, fs_l = ([] for _ in range(10))
    for l in range(DEPTH):
        p = {
            'ln_mix_pre': ln_mix_pre[l], 'ln_mix_post': ln_mix_post[l],
            'ln_ffn_pre': ln_ffn_pre[l], 'ln_ffn_post': ln_ffn_post[l], 'w_in': w_in[l],
            'lq1': da_lambda_q1[l], 'lk1': da_lambda_k1[l], 'lq2': da_lambda_q2[l], 'lk2': da_lambda_k2[l],
            'da_subln': da_subln[l], 'gdn_conv_w': gdn_conv_w[l], 'gdn_a_log': gdn_a_log[l],
            'gdn_dt_bias': gdn_dt_bias[l], 'gdn_norm': gdn_norm[l], 'w_branch_a': w_branch_a[l],
            'w_branch_b': w_branch_b[l], 'w_out': w_out[l], 'w_ffn_in': w_ffn_in[l],
            'ffn_conv_w': ffn_conv_w[l], 'ffn_conv_b': ffn_conv_b[l], 'w_down': w_down[l],
        }
        lam_init = 0.8 - 0.6 * math.exp(-0.3 * l)
        xp, k_new, v_new, c_new, s_new, f_new = decoder_layer(
            xp, p, lam_init, diff_attn_prompt, prompt_segments, zero_gdn_conv, zero_gdn, zero_ffn_conv)
        kp_l.append(k_new); vp_l.append(v_new); cp_l.append(c_new); sp_l.append(s_new); fp_l.append(f_new)
        k_past = cache_k[l][page_table].reshape(ds, -1, DA_HEADS, 2, DA_HEAD_DIM)
        v_past = cache_v[l][page_table].reshape(ds, -1, DA_HEADS, 2 * DA_HEAD_DIM)
        attn_s = functools.partial(diff_attn_sample, k_past=k_past, v_past=v_past)
        xs, k_new, v_new, c_new, s_new, f_new = decoder_layer(
            xs, p, lam_init, attn_s, sample_segments, state_gdn_conv[l], state_gdn[l], state_ffn_conv[l])
        ks_l.append(k_new); vs_l.append(v_new); cs_l.append(c_new); ss_l.append(s_new); fs_l.append(f_new)
    return (xp[:, N_META:], xs,
            jnp.stack(kp_l), jnp.stack(vp_l), jnp.stack(ks_l), jnp.stack(vs_l),
            jnp.stack(sp_l), jnp.stack(ss_l), jnp.stack(cp_l), jnp.stack(cs_l),
            jnp.stack(fp_l), jnp.stack(fs_l))
```

```python
import functools
import math

import jax
import jax.numpy as jnp
from jax import lax
from jax.experimental import pallas as pl
from jax.experimental.pallas import tpu as pltpu

F32 = jnp.float32
BF16 = jnp.bfloat16
HI = lax.Precision.HIGHEST

RMS_EPS = 1e-6
L2_EPS = 1e-6
N_META = 16
GDN_CHUNK = 64
GDN_CONV = 4
FFN_CONV = 3
INV_BLOCK = 16

LANES = 128
SUBLANES = 8
VMEM_CAP = 56 << 20


def _vmem_limit(nbytes):
    return int(min(VMEM_CAP, nbytes + (16 << 20)))


def _rms_rows(x, w):
    ms = jnp.mean(x * x, axis=-1, keepdims=True)
    return x * lax.rsqrt(ms + RMS_EPS) * w


def _dot(a, b, precision=None):
    return jnp.dot(a, b, preferred_element_type=F32, precision=precision)


def _dot_nt(a, b, precision=None):
    return lax.dot_general(a, b, (((1,), (1,)), ((), ())), preferred_element_type=F32, precision=precision)


def _dot_tn(a, b, precision=None):
    return lax.dot_general(a, b, (((0,), (0,)), ((), ())), preferred_element_type=F32, precision=precision)


def _div_pow2(x, n):
    assert n & (n - 1) == 0
    return lax.shift_right_logical(x, n.bit_length() - 1)


def _mod_pow2(x, n):
    assert n & (n - 1) == 0
    return lax.bitwise_and(x, n - 1)


def _sigmoid(x):
    return 1.0 / (1.0 + jnp.exp(-x))


def _softplus(x):
    return jnp.maximum(x, 0.0) + jnp.log1p(jnp.exp(-jnp.abs(x)))


def _gelu_tanh(x):
    c = 0.7978845608028654
    return 0.5 * x * (1.0 + jnp.tanh(c * (x + 0.044715 * (x * x * x))))


def _norm_matmul_kernel(x_ref, lnw_ref, w_ref, o_ref, xn_ref):
    @pl.when(pl.program_id(1) == 0)
    def _():
        xn_ref[...] = _rms_rows(x_ref[...], lnw_ref[...]).astype(BF16)

    o_ref[...] = _dot(xn_ref[...], w_ref[...])


def _norm_matmul(x, lnw, w, tm, tn):
    m, d = x.shape
    n = w.shape[1]
    assert m % tm == 0 and n % tn == 0
    est = 2 * (tm * d * 4 + d * tn * 2 + tm * tn * 4) + tm * d * 2
    return pl.pallas_call(
        _norm_matmul_kernel,
        grid=(m // tm, n // tn),
        in_specs=[
            pl.BlockSpec((tm, d), lambda i, j: (i, 0)),
            pl.BlockSpec((1, d), lambda i, j: (0, 0)),
            pl.BlockSpec((d, tn), lambda i, j: (0, j)),
        ],
        out_specs=pl.BlockSpec((tm, tn), lambda i, j: (i, j)),
        out_shape=jax.ShapeDtypeStruct((m, n), F32),
        scratch_shapes=[pltpu.VMEM((tm, d), BF16)],
        compiler_params=pltpu.CompilerParams(
            dimension_semantics=("arbitrary", "arbitrary"), vmem_limit_bytes=_vmem_limit(est)),
        name="in_proj",
    )(x, lnw, w)


def _diff_lambda(lqk, lam_init):
    return (jnp.exp(jnp.sum(lqk[0:1] * lqk[1:2], axis=-1, keepdims=True))
            - jnp.exp(jnp.sum(lqk[2:3] * lqk[3:4], axis=-1, keepdims=True)) + lam_init)


def _softmax_step(s, m, l, acc, v):
    m_new = jnp.maximum(m, jnp.max(s, axis=-1, keepdims=True))
    alpha = jnp.exp(m - m_new)
    p = jnp.exp(s - m_new)
    psum = p[:, 0:LANES]
    for c in range(1, s.shape[1] // LANES):
        psum = psum + p[:, c * LANES:(c + 1) * LANES]
    l = alpha * l + psum
    acc = alpha * acc + _dot(p.astype(BF16), v)
    return m_new, l, acc


def _sub_norm(o, subw, lam_init):
    return _rms_rows(o, subw) * (1.0 - lam_init)


def _attn_prompt_kernel(lqk_ref, subw_ref, q_ref, k_ref, v_ref, o_ref, kb_ref, vb_ref, *, seq, tq, lam_init):
    dh = LANES // 2
    lam = _diff_lambda(lqk_ref[...], lam_init)
    kb_ref[...] = k_ref[0].astype(BF16)
    vb_ref[...] = v_ref[0].astype(BF16)
    lane = lax.broadcasted_iota(jnp.int32, (1, LANES), 1)
    n_main = seq // tq
    tail = seq - n_main * tq

    def q_block(qs, nq, n_full, ks_last, tk_last, key_lo):
        q = q_ref[0, pl.ds(qs, nq), :] * (dh ** -0.5)
        q1 = jnp.where(lane < dh, q, 0.0).astype(BF16)
        q2 = jnp.where(lane >= dh, q, 0.0).astype(BF16)
        init = (jnp.full((nq, 1), -jnp.inf, F32), jnp.zeros((nq, LANES), F32), jnp.zeros((nq, LANES), F32),
                jnp.full((nq, 1), -jnp.inf, F32), jnp.zeros((nq, LANES), F32), jnp.zeros((nq, LANES), F32))

        def full_body(j, c):
            ks = pl.multiple_of(j * tq, tq)
            k = kb_ref[pl.ds(ks, tq), :]
            v = vb_ref[pl.ds(ks, tq), :]
            m1, l1, a1 = _softmax_step(_dot_nt(q1, k), c[0], c[1], c[2], v)
            m2, l2, a2 = _softmax_step(_dot_nt(q2, k), c[3], c[4], c[5], v)
            return (m1, l1, a1, m2, l2, a2)

        c = lax.fori_loop(0, n_full, full_body, init)
        k = kb_ref[pl.ds(ks_last, tk_last), :]
        v = vb_ref[pl.ds(ks_last, tk_last), :]
        row = qs + lax.broadcasted_iota(jnp.int32, (nq, 1), 0)
        col = ks_last + lax.broadcasted_iota(jnp.int32, (1, tk_last), 1)
        keep = col <= row
        if key_lo is not None:
            keep = jnp.logical_and(keep, col >= key_lo)
        s1 = jnp.where(keep, _dot_nt(q1, k), -jnp.inf)
        s2 = jnp.where(keep, _dot_nt(q2, k), -jnp.inf)
        m1, l1, a1 = _softmax_step(s1, c[0], c[1], c[2], v)
        m2, l2, a2 = _softmax_step(s2, c[3], c[4], c[5], v)
        o = (a1 / jnp.sum(l1, axis=-1, keepdims=True)
             - lam * (a2 / jnp.sum(l2, axis=-1, keepdims=True)))
        o_ref[0, pl.ds(qs, nq), :] = _sub_norm(o, subw_ref[...], lam_init).astype(o_ref.dtype)

    def main_body(qi, carry):
        qs = pl.multiple_of(qi * tq, tq)
        q_block(qs, tq, qi, qs, tq, None)
        return carry

    lax.fori_loop(0, n_main, main_body, 0)
    if tail:
        q_block(n_main * tq, tail, n_main, seq - LANES, LANES, n_main * tq)


def _attn_prompt(h3, lqk, subw, *, n_heads, lam_init, tq):
    bsz, seq, _ = h3.shape
    assert seq >= LANES and (seq - (seq // tq) * tq) % 16 == 0
    kern = functools.partial(_attn_prompt_kernel, seq=seq, tq=tq, lam_init=lam_init)
    blk = lambda off: pl.BlockSpec((1, seq, LANES), lambda b, h: (b, 0, off + h))
    est = 2 * (3 * seq * LANES * 4 + seq * LANES * 2) + 2 * seq * LANES * 2 + 12 * tq * tq * 4
    return pl.pallas_call(
        kern,
        grid=(bsz, n_heads),
        in_specs=[
            pl.BlockSpec(lqk.shape, lambda b, h: (0, 0)),
            pl.BlockSpec((1, LANES), lambda b, h: (0, 0)),
            blk(0), blk(n_heads), blk(2 * n_heads),
        ],
        out_specs=pl.BlockSpec((1, seq, LANES), lambda b, h: (b, 0, h)),
        out_shape=jax.ShapeDtypeStruct((bsz, seq, n_heads * LANES), BF16),
        scratch_shapes=[pltpu.VMEM((seq, LANES), BF16), pltpu.VMEM((seq, LANES), BF16)],
        compiler_params=pltpu.CompilerParams(
            dimension_semantics=("arbitrary", "arbitrary"), vmem_limit_bytes=_vmem_limit(est)),
        name="attn_prompt",
    )(lqk, subw, h3, h3, h3)


def _attn_sample_kernel(pt_ref, lqk_ref, subw_ref, q_ref, kn_ref, vn_ref, *rest, pps, n_heads, lam_init):
    del pt_ref
    kp_refs, vp_refs = rest[:pps], rest[pps:2 * pps]
    o_ref, qrows_ref, m_ref, l_ref, acc_ref = rest[2 * pps:]
    dec_seq, width = q_ref.shape[1], q_ref.shape[2]
    dh = LANES // 2
    n_rows = 2 * n_heads * dec_seq
    page = kp_refs[0].shape[0]
    j = pl.program_id(1)

    @pl.when(j == 0)
    def _():
        q = q_ref[0] * (dh ** -0.5)
        qt = jnp.concatenate([q] * (2 * n_heads), axis=0)
        rgrp = _div_pow2(lax.broadcasted_iota(jnp.int32, (n_rows, width), 0), dec_seq)
        cgrp = _div_pow2(lax.broadcasted_iota(jnp.int32, (n_rows, width), 1), dh)
        qrows_ref[...] = jnp.where(rgrp == cgrp, qt, 0.0).astype(BF16)
        m_ref[...] = jnp.full(m_ref.shape, -jnp.inf, F32)
        l_ref[...] = jnp.zeros(l_ref.shape, F32)
        acc_ref[...] = jnp.zeros(acc_ref.shape, F32)

    qr = qrows_ref[...]

    def update(s_list, v_list):
        smax = s_list[0]
        for s in s_list[1:]:
            smax = jnp.maximum(smax, s)
        m_old = m_ref[...]
        m_new = jnp.maximum(m_old, jnp.max(smax, axis=-1, keepdims=True))
        alpha = jnp.exp(m_old - m_new)
        l = alpha * l_ref[...]
        acc = alpha * acc_ref[...]
        for s, v in zip(s_list, v_list):
            p = jnp.exp(s - m_new)
            l = l + p
            acc = acc + _dot(p.astype(BF16), v)
        m_ref[...] = m_new
        l_ref[...] = l
        acc_ref[...] = acc

    update([_dot_nt(qr, kp[...].astype(BF16)) for kp in kp_refs],
           [vp[...].astype(BF16) for vp in vp_refs])

    @pl.when(j == pl.num_programs(1) - 1)
    def _():
        pad = jnp.zeros((page - dec_seq, width), F32)
        kn = jnp.concatenate([kn_ref[0], pad], axis=0).astype(BF16)
        vn = jnp.concatenate([vn_ref[0], pad], axis=0).astype(BF16)
        qpos = _mod_pow2(lax.broadcasted_iota(jnp.int32, (n_rows, 1), 0), dec_seq)
        col = lax.broadcasted_iota(jnp.int32, (1, page), 1)
        s = jnp.where(col <= qpos, _dot_nt(qr, kn), -jnp.inf)
        update([s], [vn])
        lam = _diff_lambda(lqk_ref[...], lam_init)
        inv_l = 1.0 / jnp.sum(l_ref[...], axis=-1, keepdims=True)
        acc = acc_ref[...] * inv_l
        for h in range(n_heads):
            r0 = 2 * h * dec_seq
            cols = slice(h * LANES, (h + 1) * LANES)
            o = acc[r0:r0 + dec_seq, cols] - lam * acc[r0 + dec_seq:r0 + 2 * dec_seq, cols]
            o_ref[0, :, cols] = _sub_norm(o, subw_ref[...], lam_init).astype(o_ref.dtype)


def _attn_sample(h3, cache_k, cache_v, page_table, layer, lqk, subw, *, n_heads, lam_init, pps):
    bsz, dec_seq, _ = h3.shape
    n_pages = page_table.shape[1]
    page, width = cache_k.shape[2], cache_k.shape[3]
    assert n_pages % pps == 0 and width == n_heads * LANES
    kern = functools.partial(_attn_sample_kernel, pps=pps, n_heads=n_heads, lam_init=lam_init)
    n_rows = 2 * n_heads * dec_seq

    def page_spec(i):
        return pl.BlockSpec((None, None, page, width), lambda b, j, pt: (layer, pt[b, j * pps + i], 0, 0))

    new_spec = lambda off: pl.BlockSpec((1, dec_seq, width), lambda b, j, pt: (b, 0, off))
    est = 2 * (2 * pps * page * width * 4) + 4 * n_rows * width * 4
    grid_spec = pltpu.PrefetchScalarGridSpec(
        num_scalar_prefetch=1,
        grid=(bsz, n_pages // pps),
        in_specs=[
            pl.BlockSpec(lqk.shape, lambda b, j, pt: (0, 0)),
            pl.BlockSpec((1, LANES), lambda b, j, pt: (0, 0)),
            new_spec(0), new_spec(1), new_spec(2),
        ] + [page_spec(i) for i in range(pps)] + [page_spec(i) for i in range(pps)],
        out_specs=pl.BlockSpec((1, dec_seq, width), lambda b, j, pt: (b, 0, 0)),
        scratch_shapes=[
            pltpu.VMEM((n_rows, width), BF16),
            pltpu.VMEM((n_rows, 1), F32),
            pltpu.VMEM((n_rows, LANES), F32),
            pltpu.VMEM((n_rows, width), F32),
        ],
    )
    return pl.pallas_call(
        kern,
        grid_spec=grid_spec,
        out_shape=jax.ShapeDtypeStruct((bsz, dec_seq, width), F32),
        compiler_params=pltpu.CompilerParams(
            dimension_semantics=("arbitrary", "arbitrary"), vmem_limit_bytes=_vmem_limit(est)),
        name="attn_sample",
    )(page_table, lqk, subw, h3, h3, h3, *([cache_k] * pps), *([cache_v] * pps))


def _nilpotent_inverse(lm, order, eye):
    t = eye - lm
    p = lm
    k = 2
    while k < order:
        p = _dot(p, p, HI)
        t = _dot(t, eye + p, HI)
        k *= 2
    return t


def _unit_lower_inverse(lm, ri, ci):
    c = lm.shape[0]
    eye = jnp.where(ri == ci, 1.0, 0.0)
    if c <= INV_BLOCK:
        return _nilpotent_inverse(lm, c, eye)
    same = _div_pow2(ri, INV_BLOCK) == _div_pow2(ci, INV_BLOCK)
    diag = jnp.where(same, lm, 0.0)
    t_diag = _nilpotent_inverse(diag, INV_BLOCK, eye)
    n = _dot(t_diag, lm - diag, HI)
    return _dot(_nilpotent_inverse(n, c // INV_BLOCK, eye), t_diag, HI)


def _conv_silu(win, w):
    taps = w.shape[0]
    acc = win[SUBLANES:] * w[taps - 1:taps]
    for s in range(1, taps):
        acc = acc + pltpu.roll(win, s, 0)[SUBLANES:] * w[taps - 1 - s:taps - s]
    return acc * _sigmoid(acc)


def _l2_rows(x):
    return x * lax.rsqrt(jnp.sum(x * x, axis=-1, keepdims=True) + L2_EPS)


def _gdn_kernel(alog_ref, dtb_ref, nw_ref, cwq_ref, cwk_ref, cwv_ref, csq_ref, csk_ref, csv_ref,
                xq_ref, xk_ref, xv_ref, z_ref, ba_ref, s0_ref, o_ref, sout_ref,
                winq_ref, wink_ref, winv_ref, *, seq, n_heads, head_c, main_c):
    h = pl.program_id(1)
    dk = LANES
    lane = lax.broadcasted_iota(jnp.int32, (1, LANES), 1)
    pick = lambda x, idx: jnp.sum(jnp.where(lane == idx, x, 0.0), axis=-1, keepdims=True)
    neg_a = -jnp.exp(pick(alog_ref[...], h))
    dtb = pick(dtb_ref[...], h)
    for win, cs, x in ((winq_ref, csq_ref, xq_ref), (wink_ref, csk_ref, xk_ref), (winv_ref, csv_ref, xv_ref)):
        win[0:SUBLANES, :] = cs[0]
        win[SUBLANES:SUBLANES + seq, :] = x[0]
    cw = (cwq_ref[0:GDN_CONV, :], cwk_ref[0:GDN_CONV, :], cwv_ref[0:GDN_CONV, :])

    def chunk(start, c, s):
        q = _conv_silu(winq_ref[pl.ds(start, c + SUBLANES), :], cw[0])
        k = _conv_silu(wink_ref[pl.ds(start, c + SUBLANES), :], cw[1])
        v = _conv_silu(winv_ref[pl.ds(start, c + SUBLANES), :], cw[2])
        qn = _l2_rows(q) * (dk ** -0.5)
        kn = _l2_rows(k)
        ba = ba_ref[0, pl.ds(start, c), :]
        beta = _sigmoid(pick(ba, h))
        g = neg_a * _softplus(pick(ba, h + n_heads) + dtb)
        ri = lax.broadcasted_iota(jnp.int32, (c, c), 0)
        ci = lax.broadcasted_iota(jnp.int32, (c, c), 1)
        causal = ci <= ri
        gc = _dot(jnp.where(causal, 1.0, 0.0), jnp.broadcast_to(g, (c, LANES)), HI)[:, 0:1]
        g_row = _dot(jnp.ones((c, c), F32), jnp.where(ri <= ci, g, 0.0), HI)
        decay = jnp.exp(jnp.where(causal, gc - g_row, -jnp.inf))
        kb = kn * beta
        lm = jnp.where(ci < ri, _dot_nt(kb, kn, HI) * decay, 0.0)
        t_inv = _unit_lower_inverse(lm, ri, ci)
        u = _dot(t_inv, v * beta, HI)
        w = _dot(t_inv, kb * jnp.exp(gc), HI)
        qk = _dot_nt(qn, kn, HI) * decay
        g_last = jnp.sum(g, axis=0, keepdims=True)
        v_new = u - _dot(w, s, HI)
        o = _dot(qn * jnp.exp(gc), s, HI) + _dot(qk, v_new, HI)
        s = s * jnp.exp(g_last) + _dot_tn(kn * jnp.exp(g_last - gc), v_new, HI)
        z = z_ref[0, pl.ds(start, c), :]
        o_ref[0, pl.ds(start, c), :] = (_rms_rows(o, nw_ref[...]) * (z * _sigmoid(z))).astype(o_ref.dtype)
        return s

    s = chunk(0, head_c, s0_ref[0, 0])
    n_main = (seq - head_c) // main_c

    def body(i, s):
        return chunk(pl.multiple_of(head_c + i * main_c, math.gcd(head_c, main_c)), main_c, s)

    if n_main:
        s = lax.fori_loop(0, n_main, body, s)
    sout_ref[0, 0] = s


def _gdn(h3, offs, conv_w, conv_state, s0, alog, dtb, norm_w, *, n_heads, head_c, main_c, out_dtype):
    bsz, seq, _ = h3.shape
    assert (seq - head_c) % main_c == 0
    oq, ok, ov, oz, oba = offs
    kern = functools.partial(_gdn_kernel, seq=seq, n_heads=n_heads, head_c=head_c, main_c=main_c)
    row = lambda: pl.BlockSpec((1, LANES), lambda b, h: (0, 0))
    cwspec = lambda off: pl.BlockSpec((SUBLANES, LANES), lambda b, h: (0, off + h))
    csspec = lambda off: pl.BlockSpec((1, SUBLANES, LANES), lambda b, h: (b, 0, off + h))
    xspec = lambda off: pl.BlockSpec((1, seq, LANES), lambda b, h: (b, 0, off + h))
    sspec = pl.BlockSpec((1, 1, LANES, LANES), lambda b, h: (b, h, 0, 0))
    est = 2 * (5 * seq * LANES * 4 + seq * LANES * 4) + 3 * (seq + SUBLANES) * LANES * 4
    return pl.pallas_call(
        kern,
        grid=(bsz, n_heads),
        in_specs=[row(), row(), row(),
                  cwspec(0), cwspec(n_heads), cwspec(2 * n_heads),
                  csspec(0), csspec(n_heads), csspec(2 * n_heads),
                  xspec(oq), xspec(ok), xspec(ov), xspec(oz),
                  pl.BlockSpec((1, seq, LANES), lambda b, h: (b, 0, oba)),
                  sspec],
        out_specs=[pl.BlockSpec((1, seq, LANES), lambda b, h: (b, 0, h)), sspec],
        out_shape=[jax.ShapeDtypeStruct((bsz, seq, n_heads * LANES), out_dtype),
                   jax.ShapeDtypeStruct(s0.shape, F32)],
        scratch_shapes=[pltpu.VMEM((seq + SUBLANES, LANES), F32)] * 3,
        compiler_params=pltpu.CompilerParams(
            dimension_semantics=("arbitrary", "arbitrary"), vmem_limit_bytes=_vmem_limit(est)),
        name="gdn",
    )(alog, dtb, norm_w, conv_w, conv_w, conv_w, conv_state, conv_state, conv_state,
      h3, h3, h3, h3, h3, s0)


def _merge_kernel(x_ref, oa_ref, ob_ref, ga_ref, gb_ref, wa_ref, wb_ref, wo_ref, lnw_ref, o_ref):
    ya = _dot(oa_ref[...].astype(BF16), wa_ref[...])
    yb = _dot(ob_ref[...].astype(BF16), wb_ref[...])
    mix = _sigmoid(ga_ref[...]) * ya + _sigmoid(gb_ref[...]) * yb
    y = _dot(mix.astype(BF16), wo_ref[...])
    o_ref[...] = x_ref[...] + _rms_rows(y, lnw_ref[...])


def _merge(x, oa, ob, hflat, gate_blk, wa, wb, wo, lnw, tm):
    m, d = x.shape
    wbr = oa.shape[1]
    assert m % tm == 0
    est = 2 * (4 * tm * d * 4 + 2 * tm * wbr * 4 + (2 * wbr * d + d * d) * 2) + 4 * tm * d * 4
    const = lambda shp: pl.BlockSpec(shp, lambda i: (0, 0))
    return pl.pallas_call(
        _merge_kernel,
        grid=(m // tm,),
        in_specs=[
            pl.BlockSpec((tm, d), lambda i: (i, 0)),
            pl.BlockSpec((tm, wbr), lambda i: (i, 0)),
            pl.BlockSpec((tm, wbr), lambda i: (i, 0)),
            pl.BlockSpec((tm, d), lambda i: (i, gate_blk)),
            pl.BlockSpec((tm, d), lambda i: (i, gate_blk + 1)),
            const(wa.shape), const(wb.shape), const(wo.shape), const((1, d)),
        ],
        out_specs=pl.BlockSpec((tm, d), lambda i: (i, 0)),
        out_shape=jax.ShapeDtypeStruct((m, d), F32),
        compiler_params=pltpu.CompilerParams(
            dimension_semantics=("arbitrary",), vmem_limit_bytes=_vmem_limit(est)),
        name="merge",
    )(x, oa, ob, hflat, hflat, wa, wb, wo, lnw)


def _ffn_kernel(*refs, long_seq, tiles_per_seq):
    if long_seq:
        (x_ref, xprev_ref, st_ref, lnpre_ref, lnpost_ref, wup_ref, wg_ref, cw_ref, cb_ref, wd_ref,
         o_ref, tail_ref, xe_ref, acc_ref) = refs
    else:
        (x_ref, st1_ref, st2_ref, lnpre_ref, lnpost_ref, wup_ref, wg_ref, cw_ref, cb_ref, wd_ref,
         o_ref, tail_ref, xe_ref, acc_ref) = refs
    i = pl.program_id(0)
    c = pl.program_id(1)
    tm = x_ref.shape[0]

    @pl.when(c == 0)
    def _():
        xn = _rms_rows(x_ref[...], lnpre_ref[...])
        if long_seq:
            xn = jnp.concatenate([_rms_rows(xprev_ref[...], lnpre_ref[...]), xn], axis=0)
        xe_ref[...] = xn.astype(BF16)
        acc_ref[...] = jnp.zeros(acc_ref.shape, F32)

    xe = xe_ref[...]
    up = _dot(xe, wup_ref[...])
    gate = _dot(xe, wg_ref[...])
    if long_seq:
        first = (i % tiles_per_seq) == 0
        halo = jnp.where(first, st_ref[0], gate[0:SUBLANES])
        gate = jnp.concatenate([halo, gate[SUBLANES:]], axis=0)
        g1 = pltpu.roll(gate, 1, 0)[SUBLANES:]
        g2 = pltpu.roll(gate, 2, 0)[SUBLANES:]
        g0 = gate[SUBLANES:]
        up = up[SUBLANES:]
        tail_ref[0] = gate[tm:]
    else:
        pos = _mod_pow2(lax.broadcasted_iota(jnp.int32, (tm, 1), 0), SUBLANES)
        g1 = jnp.where(pos >= 1, pltpu.roll(gate, 1, 0), st1_ref[...])
        g2 = jnp.where(pos >= 2, pltpu.roll(gate, 2, 0), st2_ref[...])
        g0 = gate
        tail_ref[...] = gate
    cw = cw_ref[...]
    y = cw[0:1] * g2 + cw[1:2] * g1 + cw[2:3] * g0 + cb_ref[...]
    acc_ref[...] += _dot((_gelu_tanh(y) * up).astype(BF16), wd_ref[...])

    @pl.when(c == pl.num_programs(1) - 1)
    def _():
        o_ref[...] = x_ref[...] + _rms_rows(acc_ref[...], lnpost_ref[...])


def _ffn(x, states, lnpre, lnpost, w_in, conv_w, conv_b, w_down, *, tm, tc, long_seq, seq):
    m, d = x.shape
    dff = w_down.shape[0]
    assert m % tm == 0 and dff % tc == 0
    nck = dff // tc
    halo = SUBLANES if long_seq else 0
    tiles_per_seq = seq // tm if long_seq else 1
    kern = functools.partial(_ffn_kernel, long_seq=long_seq, tiles_per_seq=tiles_per_seq)
    row = lambda n: pl.BlockSpec((1, n), lambda i, c: (0, 0))
    if long_seq:
        assert seq % tm == 0 and tm % SUBLANES == 0
        bpt = tm // SUBLANES
        state_specs = [
            pl.BlockSpec((SUBLANES, d), lambda i, c: (jnp.maximum(i * bpt - 1, 0), 0)),
            pl.BlockSpec((1, SUBLANES, tc), lambda i, c: (i // tiles_per_seq, 0, c)),
        ]
        state_args = [x, states[0]]
        tail_spec = pl.BlockSpec((1, SUBLANES, tc), lambda i, c: (i, 0, c))
        tail_shape = jax.ShapeDtypeStruct((m // tm, SUBLANES, dff), F32)
    else:
        state_specs = [pl.BlockSpec((tm, tc), lambda i, c: (i, c))] * 2
        state_args = list(states)
        tail_spec = pl.BlockSpec((tm, tc), lambda i, c: (i, c))
        tail_shape = jax.ShapeDtypeStruct((m, dff), F32)
    est = (2 * (2 * tm * d * 4 + 3 * d * tc * 2 + 3 * tm * tc * 4) + (tm + halo) * d * 2 + tm * d * 4
           + 8 * (tm + halo) * tc * 4)
    return pl.pallas_call(
        kern,
        grid=(m // tm, nck),
        in_specs=[pl.BlockSpec((tm, d), lambda i, c: (i, 0))] + state_specs + [
            row(d), row(d),
            pl.BlockSpec((d, tc), lambda i, c: (0, c)),
            pl.BlockSpec((d, tc), lambda i, c: (0, nck + c)),
            pl.BlockSpec((SUBLANES, tc), lambda i, c: (0, c)),
            pl.BlockSpec((1, tc), lambda i, c: (0, c)),
            pl.BlockSpec((tc, d), lambda i, c: (c, 0)),
        ],
        out_specs=[pl.BlockSpec((tm, d), lambda i, c: (i, 0)), tail_spec],
        out_shape=[jax.ShapeDtypeStruct((m, d), F32), tail_shape],
        scratch_shapes=[pltpu.VMEM((tm + halo, d), BF16), pltpu.VMEM((tm, d), F32)],
        compiler_params=pltpu.CompilerParams(
            dimension_semantics=("arbitrary", "arbitrary"), vmem_limit_bytes=_vmem_limit(est)),
        name="ffn",
    )(x, *state_args, lnpre, lnpost, w_in, w_in, conv_w, conv_b, w_down)


def _pad_rows_front(a, rows):
    return jnp.pad(a, ((0, 0), (rows - a.shape[1], 0), (0, 0)))


def _pad_rows_back(a, rows):
    return jnp.pad(a, ((0, rows - a.shape[0]), (0, 0)))


def _lane_row(v):
    return jnp.pad(v.astype(F32), (0, LANES - v.shape[0]))[None, :]


def kernel(x_prompt, x_sample, cache_k, cache_v, page_table, state_gdn, state_gdn_conv, state_ffn_conv, meta_tokens, ln_mix_pre, ln_mix_post, ln_ffn_pre, ln_ffn_post, w_in, da_lambda_q1, da_lambda_k1, da_lambda_q2, da_lambda_k2, da_subln, gdn_conv_w, gdn_a_log, gdn_dt_bias, gdn_norm, w_branch_a, w_branch_b, w_out, w_ffn_in, ffn_conv_w, ffn_conv_b, w_down):
    bp, seq, d = x_prompt.shape
    ds, dseq, _ = x_sample.shape
    depth = w_in.shape[0]
    da_heads, da_hd = cache_k.shape[3], cache_k.shape[5]
    gdn_heads, gdn_dk, gdn_dv = state_gdn.shape[2:]
    assert da_heads == gdn_heads and 2 * da_hd == LANES and gdn_dk == LANES and gdn_dv == LANES
    nh = da_heads
    wq = nh * LANES
    dff = w_down.shape[1]
    n_meta = meta_tokens.shape[0]
    tp = n_meta + seq
    assert dseq == SUBLANES

    widths = (wq, wq, wq, 3 * wq, wq, nh, nh, d, d)
    cuts = [0]
    for wdt in widths:
        cuts.append(cuts[-1] + wdt)
    seg = lambda w, k: w[..., cuts[k]:cuts[k + 1]]
    gate_col = 6 * wq
    assert gate_col % d == 0
    offs_gdn = (3 * nh, 4 * nh, 5 * nh, (6 * wq + 2 * d) // LANES, (7 * wq + 2 * d) // LANES)
    w_in_r = jnp.concatenate(
        [seg(w_in, 0), seg(w_in, 1), seg(w_in, 2), seg(w_in, 3), seg(w_in, 7), seg(w_in, 8), seg(w_in, 4),
         seg(w_in, 5), seg(w_in, 6), jnp.zeros((depth, d, LANES - 2 * nh), w_in.dtype)], axis=-1).astype(BF16)
    n_in = w_in_r.shape[-1]
    tn = 1920
    assert n_in % tn == 0

    wa_b, wb_b, wo_b = w_branch_a.astype(BF16), w_branch_b.astype(BF16), w_out.astype(BF16)
    wffn_b, wdown_b = w_ffn_in.astype(BF16), w_down.astype(BF16)
    gconv_w = jnp.pad(gdn_conv_w, ((0, 0), (0, SUBLANES - GDN_CONV), (0, 0)))
    fconv_w = jnp.pad(ffn_conv_w, ((0, 0), (0, SUBLANES - FFN_CONV), (0, 0)))
    lqk = jnp.stack([da_lambda_q1, da_lambda_k1, da_lambda_q2, da_lambda_k2], axis=1).astype(F32)

    xp = jnp.concatenate([jnp.broadcast_to(meta_tokens[None], (bp, n_meta, d)), x_prompt], axis=1).reshape(bp * tp, d)
    xs = x_sample.reshape(ds * dseq, d)
    ck = cache_k.reshape(cache_k.shape[:3] + (wq,))
    cv = cache_v.reshape(cache_v.shape[:3] + (wq,))
    zero_gconv = jnp.zeros((bp, SUBLANES, 3 * wq), F32)
    zero_s = jnp.zeros((bp, nh, gdn_dk, gdn_dv), F32)
    zero_fconv = jnp.zeros((bp, SUBLANES, dff), F32)
    tm_p = tp // 2
    tm_merge = tp // 6
    assert tp % 6 == 0 and tm_merge % SUBLANES == 0

    outs = [[] for _ in range(10)]
    for l in range(depth):
        lam_init = 0.8 - 0.6 * math.exp(-0.3 * l)
        row = lambda v: v[l][None, :]
        alog, dtb = _lane_row(gdn_a_log[l]), _lane_row(gdn_dt_bias[l])

        def mixer(x, h3, oa, s_conv, s0, head_c, out_dtype, tm):
            bsz, t, _ = h3.shape
            ob, s_new = _gdn(h3, offs_gdn, gconv_w[l], s_conv, s0, alog, dtb, row(gdn_norm),
                             n_heads=nh, head_c=head_c, main_c=GDN_CHUNK, out_dtype=out_dtype)
            hflat = h3.reshape(bsz * t, n_in)
            x = _merge(x, oa.reshape(bsz * t, wq), ob.reshape(bsz * t, wq), hflat, gate_col // d,
                       wa_b[l], wb_b[l], wo_b[l], row(ln_mix_post), tm)
            k_new = h3[:, :, wq:2 * wq].reshape(bsz, t, nh, 2, da_hd)
            v_new = h3[:, :, 2 * wq:3 * wq].reshape(bsz, t, nh, 2 * da_hd)
            c_new = h3[:, t - (GDN_CONV - 1):, 3 * wq:6 * wq]
            return x, k_new, v_new, c_new, s_new

        hp3 = _norm_matmul(xp, row(ln_mix_pre), w_in_r[l], tm_p, tn).reshape(bp, tp, n_in)
        oa = _attn_prompt(hp3, lqk[l], row(da_subln), n_heads=nh, lam_init=lam_init, tq=256)
        xp, k_new, v_new, c_new, s_new = mixer(xp, hp3, oa, zero_gconv, zero_s, n_meta, BF16, tm_merge)
        xp, tail = _ffn(xp, [zero_fconv], row(ln_ffn_pre), row(ln_ffn_post), wffn_b[l], fconv_w[l],
                        row(ffn_conv_b), wdown_b[l], tm=tm_p, tc=256, long_seq=True, seq=tp)
        f_new = tail.reshape(bp, tp // tm_p, SUBLANES, dff)[:, -1, SUBLANES - (FFN_CONV - 1):]
        for lst, val in zip(outs[0::2], (k_new, v_new, s_new, c_new, f_new)):
            lst.append(val)

        hs3 = _norm_matmul(xs, row(ln_mix_pre), w_in_r[l], ds * dseq, tn).reshape(ds, dseq, n_in)
        oa = _attn_sample(hs3, ck, cv, page_table, l, lqk[l], row(da_subln), n_heads=nh, lam_init=lam_init, pps=8)
        xs, k_new, v_new, c_new, s_new = mixer(
            xs, hs3, oa, _pad_rows_front(state_gdn_conv[l], SUBLANES), state_gdn[l], dseq, F32, ds * dseq)
        buf = state_ffn_conv[l]
        st1 = jnp.pad(buf[:, 1:2], ((0, 0), (0, dseq - 1), (0, 0))).reshape(ds * dseq, dff)
        st2 = jnp.pad(buf, ((0, 0), (0, dseq - 2), (0, 0))).reshape(ds * dseq, dff)
        xs, gate_rows = _ffn(xs, [st1, st2], row(ln_ffn_pre), row(ln_ffn_post), wffn_b[l], fconv_w[l],
                             row(ffn_conv_b), wdown_b[l], tm=ds * dseq, tc=256, long_seq=False, seq=dseq)
        f_new = gate_rows.reshape(ds, dseq, dff)[:, dseq - (FFN_CONV - 1):]
        for lst, val in zip(outs[1::2], (k_new, v_new, s_new, c_new, f_new)):
            lst.append(val)

    kp_l, ks_l, vp_l, vs_l, sp_l, ss_l, cp_l, cs_l, fp_l, fs_l = outs
    return (xp.reshape(bp, tp, d)[:, n_meta:], xs.reshape(ds, dseq, d),
            jnp.stack(kp_l), jnp.stack(vp_l), jnp.stack(ks_l), jnp.stack(vs_l),
            jnp.stack(sp_l), jnp.stack(ss_l), jnp.stack(cp_l), jnp.stack(cs_l),
            jnp.stack(fp_l), jnp.stack(fs_l))
```

```python
import functools
import math

import jax
import jax.numpy as jnp
from jax import lax
from jax.experimental import pallas as pl
from jax.experimental.pallas import tpu as pltpu

F32 = jnp.float32
BF16 = jnp.bfloat16
HI = lax.Precision.HIGHEST

RMS_EPS = 1e-6
L2_EPS = 1e-6
N_META = 16
GDN_CHUNK = 64
GDN_GROUP = 256
GDN_CONV = 4
FFN_CONV = 3
INV_BLOCK = 16

LANES = 128
SUBLANES = 8
VMEM_CAP = 56 << 20


def _vmem_limit(nbytes):
    return int(min(VMEM_CAP, nbytes + (16 << 20)))


def _rms_rows(x, w):
    ms = jnp.mean(x * x, axis=-1, keepdims=True)
    return x * lax.rsqrt(ms + RMS_EPS) * w


def _dot(a, b, precision=None):
    return jnp.dot(a, b, preferred_element_type=F32, precision=precision)


def _dot_nt(a, b, precision=None):
    return lax.dot_general(a, b, (((1,), (1,)), ((), ())), preferred_element_type=F32, precision=precision)


def _dot_tn(a, b, precision=None):
    return lax.dot_general(a, b, (((0,), (0,)), ((), ())), preferred_element_type=F32, precision=precision)


def _div_pow2(x, n):
    assert n & (n - 1) == 0
    return lax.shift_right_logical(x, jnp.asarray(n.bit_length() - 1, x.dtype))


def _mod_pow2(x, n):
    assert n & (n - 1) == 0
    return lax.bitwise_and(x, jnp.asarray(n - 1, x.dtype))


def _sigmoid(x):
    return 1.0 / (1.0 + jnp.exp(-x))


def _softplus(x):
    return jnp.maximum(x, 0.0) + jnp.log1p(jnp.exp(-jnp.abs(x)))


def _gelu_tanh(x):
    c = 0.7978845608028654
    return 0.5 * x * (1.0 + jnp.tanh(c * (x + 0.044715 * (x * x * x))))


def _norm_matmul_kernel(x_ref, lnw_ref, w_ref, o_ref, xn_ref):
    @pl.when(pl.program_id(1) == 0)
    def _():
        xn_ref[...] = _rms_rows(x_ref[...], lnw_ref[...]).astype(BF16)

    o_ref[...] = _dot(xn_ref[...], w_ref[...])


def _norm_matmul(x, lnw, w, tm, tn):
    m, d = x.shape
    n = w.shape[1]
    assert m % tm == 0 and n % tn == 0
    est = 2 * (tm * d * 4 + d * tn * 2 + tm * tn * 4) + tm * d * 2
    return pl.pallas_call(
        _norm_matmul_kernel,
        grid=(m // tm, n // tn),
        in_specs=[
            pl.BlockSpec((tm, d), lambda i, j: (i, 0)),
            pl.BlockSpec((1, d), lambda i, j: (0, 0)),
            pl.BlockSpec((d, tn), lambda i, j: (0, j)),
        ],
        out_specs=pl.BlockSpec((tm, tn), lambda i, j: (i, j)),
        out_shape=jax.ShapeDtypeStruct((m, n), F32),
        scratch_shapes=[pltpu.VMEM((tm, d), BF16)],
        compiler_params=pltpu.CompilerParams(
            dimension_semantics=("arbitrary", "arbitrary"), vmem_limit_bytes=_vmem_limit(est)),
        name="in_proj",
    )(x, lnw, w)


def _diff_lambda(lqk, lam_init):
    return (jnp.exp(jnp.sum(lqk[0:1] * lqk[1:2], axis=-1, keepdims=True))
            - jnp.exp(jnp.sum(lqk[2:3] * lqk[3:4], axis=-1, keepdims=True)) + lam_init)


def _sub_norm(o, subw, lam_init):
    return _rms_rows(o, subw) * (1.0 - lam_init)


def _attn_prompt_kernel(lqk_ref, subw_ref, q_ref, k_ref, v_ref, o_ref, kb_ref, vb_ref, *, seq, tq, lam_init):
    dh = LANES // 2
    lam = _diff_lambda(lqk_ref[...], lam_init)
    kb_ref[...] = k_ref[0].astype(BF16)
    vb_ref[...] = v_ref[0].astype(BF16)
    lane = lax.broadcasted_iota(jnp.int32, (1, LANES), 1)
    n_main = seq // tq
    tail = seq - n_main * tq

    def lane_sum(p):
        part = p[:, 0:LANES]
        for c in range(1, p.shape[1] // LANES):
            part = part + p[:, c * LANES:(c + 1) * LANES]
        return jnp.sum(part, axis=-1, keepdims=True)

    def q_block(qs, nq, nk_full, ks_last, tk_last, key_lo):
        q = q_ref[0, qs:qs + nq, :] * (dh ** -0.5)
        row = qs + lax.broadcasted_iota(jnp.int32, (nq, 1), 0)
        col = ks_last + lax.broadcasted_iota(jnp.int32, (1, tk_last), 1)
        keep = col <= row
        if key_lo is not None:
            keep = jnp.logical_and(keep, col >= key_lo)
        k_last, v_last = kb_ref[ks_last:ks_last + tk_last, :], vb_ref[ks_last:ks_last + tk_last, :]
        outs = []
        for qm in (jnp.where(lane < dh, q, 0.0).astype(BF16), jnp.where(lane >= dh, q, 0.0).astype(BF16)):
            s_last = jnp.where(keep, _dot_nt(qm, k_last), -jnp.inf)
            m = jnp.max(s_last, axis=-1, keepdims=True)
            if nk_full:
                s_full = _dot_nt(qm, kb_ref[0:nk_full, :])
                m = jnp.maximum(m, jnp.max(s_full, axis=-1, keepdims=True))
            p = jnp.exp(s_last - m)
            l = lane_sum(p)
            acc = _dot(p.astype(BF16), v_last)
            if nk_full:
                p = jnp.exp(s_full - m)
                l = l + lane_sum(p)
                acc = acc + _dot(p.astype(BF16), vb_ref[0:nk_full, :])
            outs.append(acc / l)
        o = outs[0] - lam * outs[1]
        o_ref[0, qs:qs + nq, :] = _sub_norm(o, subw_ref[...], lam_init).astype(o_ref.dtype)

    for qi in range(n_main):
        q_block(qi * tq, tq, qi * tq, qi * tq, tq, None)
    if tail:
        q_block(n_main * tq, tail, n_main * tq, seq - LANES, LANES, n_main * tq)


def _attn_prompt(h3, lqk, subw, *, n_heads, lam_init, tq):
    bsz, seq, _ = h3.shape
    assert seq >= LANES and (seq - (seq // tq) * tq) % 16 == 0
    kern = functools.partial(_attn_prompt_kernel, seq=seq, tq=tq, lam_init=lam_init)
    blk = lambda off: pl.BlockSpec((1, seq, LANES), lambda b, h: (b, 0, off + h))
    est = 2 * (3 * seq * LANES * 4 + seq * LANES * 2) + 2 * seq * LANES * 2 + 4 * tq * seq * 4
    return pl.pallas_call(
        kern,
        grid=(bsz, n_heads),
        in_specs=[
            pl.BlockSpec(lqk.shape, lambda b, h: (0, 0)),
            pl.BlockSpec((1, LANES), lambda b, h: (0, 0)),
            blk(0), blk(n_heads), blk(2 * n_heads),
        ],
        out_specs=pl.BlockSpec((1, seq, LANES), lambda b, h: (b, 0, h)),
        out_shape=jax.ShapeDtypeStruct((bsz, seq, n_heads * LANES), BF16),
        scratch_shapes=[pltpu.VMEM((seq, LANES), BF16), pltpu.VMEM((seq, LANES), BF16)],
        compiler_params=pltpu.CompilerParams(
            dimension_semantics=("arbitrary", "arbitrary"), vmem_limit_bytes=_vmem_limit(est)),
        name="attn_prompt",
    )(lqk, subw, h3, h3, h3)


def _attn_sample_kernel(pt_ref, lqk_ref, subw_ref, q_ref, kn_ref, vn_ref, *rest, pps, n_heads, lam_init):
    del pt_ref
    kp_refs, vp_refs = rest[:pps], rest[pps:2 * pps]
    o_ref, qrows_ref, m_ref, l_ref, acc_ref = rest[2 * pps:]
    dec_seq, width = q_ref.shape[1], q_ref.shape[2]
    dh = LANES // 2
    n_rows = 2 * n_heads * dec_seq
    page = kp_refs[0].shape[1]
    j = pl.program_id(1)

    @pl.when(j == 0)
    def _():
        q = q_ref[0] * (dh ** -0.5)
        qt = jnp.concatenate([q] * (2 * n_heads), axis=0)
        rgrp = _div_pow2(lax.broadcasted_iota(jnp.int32, (n_rows, width), 0), dec_seq)
        cgrp = _div_pow2(lax.broadcasted_iota(jnp.int32, (n_rows, width), 1), dh)
        qrows_ref[...] = jnp.where(rgrp == cgrp, qt, 0.0).astype(BF16)
        m_ref[...] = jnp.full(m_ref.shape, -jnp.inf, F32)
        l_ref[...] = jnp.zeros(l_ref.shape, F32)
        acc_ref[...] = jnp.zeros(acc_ref.shape, F32)

    qr = qrows_ref[...]

    def update(s_list, v_list):
        smax = s_list[0]
        for s in s_list[1:]:
            smax = jnp.maximum(smax, s)
        m_old = m_ref[...]
        m_new = jnp.maximum(m_old, jnp.max(smax, axis=-1, keepdims=True))
        alpha = jnp.exp(m_old - m_new)
        l = alpha * l_ref[...]
        acc = alpha * acc_ref[...]
        for s, v in zip(s_list, v_list):
            p = jnp.exp(s - m_new)
            l = l + p
            acc = acc + _dot(p.astype(BF16), v)
        m_ref[...] = m_new
        l_ref[...] = l
        acc_ref[...] = acc

    def value_page(vp):
        return jnp.concatenate(
            [vp[pl.ds(hh, page, stride=n_heads), :] for hh in range(n_heads)], axis=1).astype(BF16)

    update([_dot(qr, kp[...].astype(BF16)) for kp in kp_refs], [value_page(vp) for vp in vp_refs])

    @pl.when(j == pl.num_programs(1) - 1)
    def _():
        pad = jnp.zeros((page - dec_seq, width), F32)
        kn = jnp.concatenate([kn_ref[0], pad], axis=0).astype(BF16)
        vn = jnp.concatenate([vn_ref[0], pad], axis=0).astype(BF16)
        qpos = _mod_pow2(lax.broadcasted_iota(jnp.int32, (n_rows, 1), 0), dec_seq)
        col = lax.broadcasted_iota(jnp.int32, (1, page), 1)
        s = jnp.where(col <= qpos, _dot_nt(qr, kn), -jnp.inf)
        update([s], [vn])
        lam = _diff_lambda(lqk_ref[...], lam_init)
        inv_l = 1.0 / jnp.sum(l_ref[...], axis=-1, keepdims=True)
        acc = acc_ref[...] * inv_l
        for h in range(n_heads):
            r0 = 2 * h * dec_seq
            cols = slice(h * LANES, (h + 1) * LANES)
            o = acc[r0:r0 + dec_seq, cols] - lam * acc[r0 + dec_seq:r0 + 2 * dec_seq, cols]
            o_ref[0, :, cols] = _sub_norm(o, subw_ref[...], lam_init).astype(o_ref.dtype)


def _attn_sample(h3, cache_kt, cache_v, page_table, layer, lqk, subw, *, n_heads, lam_init, pps):
    bsz, dec_seq, _ = h3.shape
    n_pages = page_table.shape[1]
    width, page = cache_kt.shape[2], cache_kt.shape[3]
    assert n_pages % pps == 0 and width == n_heads * LANES and cache_v.shape[2:] == (page * n_heads, LANES)
    kern = functools.partial(_attn_sample_kernel, pps=pps, n_heads=n_heads, lam_init=lam_init)
    n_rows = 2 * n_heads * dec_seq

    def page_spec(i):
        return pl.BlockSpec((None, None, width, page), lambda b, j, pt: (layer, pt[b, j * pps + i], 0, 0))

    new_spec = lambda off: pl.BlockSpec((1, dec_seq, width), lambda b, j, pt: (b, 0, off))
    est = 2 * (2 * pps * page * width * 4) + 4 * n_rows * width * 4
    grid_spec = pltpu.PrefetchScalarGridSpec(
        num_scalar_prefetch=1,
        grid=(bsz, n_pages // pps),
        in_specs=[
            pl.BlockSpec(lqk.shape, lambda b, j, pt: (0, 0)),
            pl.BlockSpec((1, LANES), lambda b, j, pt: (0, 0)),
            new_spec(0), new_spec(1), new_spec(2),
        ] + [page_spec(i) for i in range(pps)] + [page_spec(i) for i in range(pps)],
        out_specs=pl.BlockSpec((1, dec_seq, width), lambda b, j, pt: (b, 0, 0)),
        scratch_shapes=[
            pltpu.VMEM((n_rows, width), BF16),
            pltpu.VMEM((n_rows, 1), F32),
            pltpu.VMEM((n_rows, LANES), F32),
            pltpu.VMEM((n_rows, width), F32),
        ],
    )
    return pl.pallas_call(
        kern,
        grid_spec=grid_spec,
        out_shape=jax.ShapeDtypeStruct((bsz, dec_seq, width), F32),
        compiler_params=pltpu.CompilerParams(
            dimension_semantics=("arbitrary", "arbitrary"), vmem_limit_bytes=_vmem_limit(est)),
        name="attn_sample",
    )(page_table, lqk, subw, h3, h3, h3, *([cache_kt] * pps), *([cache_v] * pps))


def _mm(a, b):
    return _dot(a.astype(BF16), b.astype(BF16))


def _mm_nt(a, b):
    return _dot_nt(a.astype(BF16), b.astype(BF16))


def _nilpotent_inverse(lm, order, eye):
    t = eye - lm
    p = lm
    k = 2
    while k < order:
        p = _mm(p, p)
        t = _mm(t, eye + p)
        k *= 2
    return t


def _unit_lower_inverse(lm, ri, ci, c):
    eye = jnp.where(ri == ci, 1.0, 0.0)
    if c <= INV_BLOCK:
        return _nilpotent_inverse(lm, c, eye)
    same = _div_pow2(ri, INV_BLOCK) == _div_pow2(ci, INV_BLOCK)
    diag = jnp.where(same, lm, 0.0)
    t_diag = _nilpotent_inverse(diag, INV_BLOCK, eye)
    n = _mm(t_diag, lm - diag)
    return _mm(_nilpotent_inverse(n, c // INV_BLOCK, eye), t_diag)


def _conv_silu(win, w):
    taps = w.shape[0]
    acc = win[SUBLANES:] * w[taps - 1:taps]
    for s in range(1, taps):
        acc = acc + pltpu.roll(win, s, 0)[SUBLANES:] * w[taps - 1 - s:taps - s]
    return acc * _sigmoid(acc)


def _l2_rows(x):
    return x * lax.rsqrt(jnp.sum(x * x, axis=-1, keepdims=True) + L2_EPS)


def _gdn_kernel(alog_ref, dtb_ref, nw_ref, cwq_ref, cwk_ref, cwv_ref, csq_ref, csk_ref, csv_ref,
                xq_ref, xk_ref, xv_ref, z_ref, ba_ref, s0_ref, o_ref, sout_ref,
                winq_ref, wink_ref, winv_ref, lhs_ref, add_ref, o0_ref, dec_ref,
                *, seq, n_heads, head_c, main_c, group):
    h = pl.program_id(1)
    dk = LANES
    lane = lax.broadcasted_iota(jnp.int32, (1, LANES), 1)
    pick = lambda x, idx: jnp.sum(jnp.where(lane == idx, x, 0.0), axis=-1, keepdims=True)
    neg_a = -jnp.exp(pick(alog_ref[...], h))
    dtb = pick(dtb_ref[...], h)
    for win, cs, x in ((winq_ref, csq_ref, xq_ref), (wink_ref, csk_ref, xk_ref), (winv_ref, csv_ref, xv_ref)):
        win[0:SUBLANES, :] = cs[0]
        win[SUBLANES:SUBLANES + seq, :] = x[0]
    cw = (cwq_ref[0:GDN_CONV, :], cwk_ref[0:GDN_CONV, :], cwv_ref[0:GDN_CONV, :])

    def prep(slot, start, rows, c):
        n = rows // c
        q = _conv_silu(winq_ref[pl.ds(start, rows + SUBLANES), :], cw[0])
        k = _conv_silu(wink_ref[pl.ds(start, rows + SUBLANES), :], cw[1])
        v = _conv_silu(winv_ref[pl.ds(start, rows + SUBLANES), :], cw[2])
        qn = _l2_rows(q) * (dk ** -0.5)
        kn = _l2_rows(k)
        ba = ba_ref[0, pl.ds(start, rows), :]
        beta = _sigmoid(pick(ba, h))
        g = neg_a * _softplus(pick(ba, h + n_heads) + dtb)
        ri = lax.broadcasted_iota(jnp.int32, (rows, rows), 0)
        ci = lax.broadcasted_iota(jnp.int32, (rows, rows), 1)
        causal, strict, upper = ci <= ri, ci < ri, ri <= ci
        if n > 1:
            same = _div_pow2(ri, c) == _div_pow2(ci, c)
            causal, strict, upper = (jnp.logical_and(same, m) for m in (causal, strict, upper))
        gcb = _dot(jnp.where(causal, 1.0, 0.0), jnp.broadcast_to(g, (rows, LANES)), HI)
        gc = gcb[:, 0:1]
        if rows % LANES == 0:
            g_row = jnp.transpose(gcb)[0:1, :]
        else:
            g_row = _dot(jnp.ones((rows, rows), F32), jnp.where(upper, g, 0.0), HI)
        decay = jnp.exp(jnp.where(causal, gc - g_row, -jnp.inf))
        chunk_of_row = _div_pow2(lax.broadcasted_iota(jnp.int32, (rows, 1), 0), c)
        gl = gc[c - 1:c, :]
        for cc in range(1, n):
            gl = jnp.where(chunk_of_row == cc, gc[(cc + 1) * c - 1:(cc + 1) * c, :], gl)
        kb = kn * beta
        both = _mm_nt(jnp.concatenate([kb, qn], axis=0), kn)
        lm = jnp.where(strict, both[:rows] * decay, 0.0)
        qk = both[rows:] * decay
        t_inv = _unit_lower_inverse(lm, ri, ci, c)
        uw = _mm(t_inv, jnp.concatenate([v * beta, kb * jnp.exp(gc)], axis=1))
        qkuw = _mm(qk, uw)
        o0_ref[slot, 0:rows, :] = qkuw[:, :LANES]
        qeff = qn * jnp.exp(gc) - qkuw[:, LANES:]
        kd = kn * jnp.exp(gl - gc)
        egl = jnp.exp(gl)
        for cc in range(n):
            r = slice(cc * c, (cc + 1) * c)
            ab = _dot_tn(kd[r].astype(BF16), uw[r].astype(BF16))
            lhs_ref[slot, cc, 0:LANES, :] = -ab[:, LANES:]
            lhs_ref[slot, cc, LANES:LANES + c, :] = qeff[r]
            add_ref[slot, cc] = ab[:, :LANES]
            dec_ref[slot, cc] = jnp.broadcast_to(egl[cc * c:cc * c + 1, :], (SUBLANES, LANES))

    def finish(slot, start, rows, c, s):
        for cc in range(rows // c):
            r = _mm(lhs_ref[slot, cc, 0:LANES + c, :], s)
            o = r[LANES:] + o0_ref[slot, cc * c:(cc + 1) * c, :]
            s = s * dec_ref[slot, cc, 0:1, :] + (r[:LANES] + add_ref[slot, cc])
            z = z_ref[0, pl.ds(start + cc * c, c), :]
            o_ref[0, pl.ds(start + cc * c, c), :] = (
                _rms_rows(o, nw_ref[...]) * (z * _sigmoid(z))).astype(o_ref.dtype)
        return s

    prep(0, 0, head_c, head_c)
    s = finish(0, 0, head_c, head_c, s0_ref[0, 0])
    n_groups = (seq - head_c) // group
    if n_groups:
        gstart = lambda gi: pl.multiple_of(head_c + gi * group, math.gcd(head_c, group))
        prep(0, head_c, group, main_c)

        def body(gi, s):
            s = finish(_mod_pow2(gi, 2), gstart(gi), group, main_c, s)
            prep(_mod_pow2(gi + 1, 2), gstart(gi + 1), group, main_c)
            return s

        s = lax.fori_loop(0, n_groups - 1, body, s)
        last = n_groups - 1
        s = finish(last % 2, head_c + last * group, group, main_c, s)
    sout_ref[0, 0] = s


def _gdn(h3, offs, conv_w, conv_state, s0, alog, dtb, norm_w, *, n_heads, head_c, main_c, group, out_dtype):
    bsz, seq, _ = h3.shape
    assert (seq - head_c) % group == 0 and group % main_c == 0
    has_main = seq > head_c
    rows_max = group if has_main else head_c
    c_max = max(head_c, main_c) if has_main else head_c
    n_cc = max(rows_max // main_c, 1) if has_main else 1
    oq, ok, ov, oz, oba = offs
    kern = functools.partial(_gdn_kernel, seq=seq, n_heads=n_heads, head_c=head_c, main_c=main_c, group=group)
    row = lambda: pl.BlockSpec((1, LANES), lambda b, h: (0, 0))
    cwspec = lambda off: pl.BlockSpec((SUBLANES, LANES), lambda b, h: (0, off + h))
    csspec = lambda off: pl.BlockSpec((1, SUBLANES, LANES), lambda b, h: (b, 0, off + h))
    xspec = lambda off: pl.BlockSpec((1, seq, LANES), lambda b, h: (b, 0, off + h))
    sspec = pl.BlockSpec((1, 1, LANES, LANES), lambda b, h: (b, h, 0, 0))
    est = 2 * (5 * seq * LANES * 4 + seq * LANES * 4) + 3 * (seq + SUBLANES) * LANES * 4
    return pl.pallas_call(
        kern,
        grid=(bsz, n_heads),
        in_specs=[row(), row(), row(),
                  cwspec(0), cwspec(n_heads), cwspec(2 * n_heads),
                  csspec(0), csspec(n_heads), csspec(2 * n_heads),
                  xspec(oq), xspec(ok), xspec(ov), xspec(oz),
                  pl.BlockSpec((1, seq, LANES), lambda b, h: (b, 0, oba)),
                  sspec],
        out_specs=[pl.BlockSpec((1, seq, LANES), lambda b, h: (b, 0, h)), sspec],
        out_shape=[jax.ShapeDtypeStruct((bsz, seq, n_heads * LANES), out_dtype),
                   jax.ShapeDtypeStruct(s0.shape, F32)],
        scratch_shapes=[pltpu.VMEM((seq + SUBLANES, LANES), F32)] * 3 + [
            pltpu.VMEM((2, n_cc, LANES + c_max, LANES), F32),
            pltpu.VMEM((2, n_cc, LANES, LANES), F32),
            pltpu.VMEM((2, rows_max, LANES), F32),
            pltpu.VMEM((2, n_cc, SUBLANES, LANES), F32),
        ],
        compiler_params=pltpu.CompilerParams(
            dimension_semantics=("arbitrary", "arbitrary"), vmem_limit_bytes=_vmem_limit(est)),
        name="gdn",
    )(alog, dtb, norm_w, conv_w, conv_w, conv_w, conv_state, conv_state, conv_state,
      h3, h3, h3, h3, h3, s0)


def _merge_kernel(x_ref, oa_ref, ob_ref, ga_ref, gb_ref, wa_ref, wb_ref, wo_ref, lnw_ref, o_ref):
    ya = _dot(oa_ref[...].astype(BF16), wa_ref[...])
    yb = _dot(ob_ref[...].astype(BF16), wb_ref[...])
    mix = _sigmoid(ga_ref[...]) * ya + _sigmoid(gb_ref[...]) * yb
    y = _dot(mix.astype(BF16), wo_ref[...])
    o_ref[...] = x_ref[...] + _rms_rows(y, lnw_ref[...])


def _merge(x, oa, ob, hflat, gate_blk, wa, wb, wo, lnw, tm):
    m, d = x.shape
    wbr = oa.shape[1]
    assert m % tm == 0
    est = 2 * (4 * tm * d * 4 + 2 * tm * wbr * 4 + (2 * wbr * d + d * d) * 2) + 4 * tm * d * 4
    const = lambda shp: pl.BlockSpec(shp, lambda i: (0, 0))
    return pl.pallas_call(
        _merge_kernel,
        grid=(m // tm,),
        in_specs=[
            pl.BlockSpec((tm, d), lambda i: (i, 0)),
            pl.BlockSpec((tm, wbr), lambda i: (i, 0)),
            pl.BlockSpec((tm, wbr), lambda i: (i, 0)),
            pl.BlockSpec((tm, d), lambda i: (i, gate_blk)),
            pl.BlockSpec((tm, d), lambda i: (i, gate_blk + 1)),
            const(wa.shape), const(wb.shape), const(wo.shape), const((1, d)),
        ],
        out_specs=pl.BlockSpec((tm, d), lambda i: (i, 0)),
        out_shape=jax.ShapeDtypeStruct((m, d), F32),
        compiler_params=pltpu.CompilerParams(
            dimension_semantics=("arbitrary",), vmem_limit_bytes=_vmem_limit(est)),
        name="merge",
    )(x, oa, ob, hflat, hflat, wa, wb, wo, lnw)


def _ffn_kernel(*refs, long_seq, tiles_per_seq):
    if long_seq:
        (x_ref, xprev_ref, st_ref, lnpre_ref, lnpost_ref, wup_ref, wg_ref, cw_ref, cb_ref, wd_ref,
         o_ref, tail_ref, xe_ref, acc_ref) = refs
    else:
        (x_ref, st1_ref, st2_ref, lnpre_ref, lnpost_ref, wup_ref, wg_ref, cw_ref, cb_ref, wd_ref,
         o_ref, tail_ref, xe_ref, acc_ref) = refs
    i = pl.program_id(0)
    c = pl.program_id(1)
    tm = x_ref.shape[0]

    @pl.when(c == 0)
    def _():
        xn = _rms_rows(x_ref[...], lnpre_ref[...])
        if long_seq:
            xn = jnp.concatenate([_rms_rows(xprev_ref[...], lnpre_ref[...]), xn], axis=0)
        xe_ref[...] = xn.astype(BF16)
        acc_ref[...] = jnp.zeros(acc_ref.shape, F32)

    xe = xe_ref[...]
    up = _dot(xe, wup_ref[...])
    gate = _dot(xe, wg_ref[...])
    if long_seq:
        first = (i % tiles_per_seq) == 0
        halo = jnp.where(first, st_ref[0], gate[0:SUBLANES])
        gate = jnp.concatenate([halo, gate[SUBLANES:]], axis=0)
        g1 = pltpu.roll(gate, 1, 0)[SUBLANES:]
        g2 = pltpu.roll(gate, 2, 0)[SUBLANES:]
        g0 = gate[SUBLANES:]
        up = up[SUBLANES:]
        tail_ref[0] = gate[tm:]
    else:
        pos = _mod_pow2(lax.broadcasted_iota(jnp.int32, (tm, 1), 0), SUBLANES)
        g1 = jnp.where(pos >= 1, pltpu.roll(gate, 1, 0), st1_ref[...])
        g2 = jnp.where(pos >= 2, pltpu.roll(gate, 2, 0), st2_ref[...])
        g0 = gate
        tail_ref[...] = gate
    cw = cw_ref[...]
    y = cw[0:1] * g2 + cw[1:2] * g1 + cw[2:3] * g0 + cb_ref[...]
    acc_ref[...] += _dot((_gelu_tanh(y) * up).astype(BF16), wd_ref[...])

    @pl.when(c == pl.num_programs(1) - 1)
    def _():
        o_ref[...] = x_ref[...] + _rms_rows(acc_ref[...], lnpost_ref[...])


def _ffn(x, states, lnpre, lnpost, w_in, conv_w, conv_b, w_down, *, tm, tc, long_seq, seq):
    m, d = x.shape
    dff = w_down.shape[0]
    assert m % tm == 0 and dff % tc == 0
    nck = dff // tc
    halo = SUBLANES if long_seq else 0
    tiles_per_seq = seq // tm if long_seq else 1
    kern = functools.partial(_ffn_kernel, long_seq=long_seq, tiles_per_seq=tiles_per_seq)
    row = lambda n: pl.BlockSpec((1, n), lambda i, c: (0, 0))
    if long_seq:
        assert seq % tm == 0 and tm % SUBLANES == 0
        bpt = tm // SUBLANES
        state_specs = [
            pl.BlockSpec((SUBLANES, d), lambda i, c: (jnp.maximum(i * bpt - 1, 0), 0)),
            pl.BlockSpec((1, SUBLANES, tc), lambda i, c: (i // tiles_per_seq, 0, c)),
        ]
        state_args = [x, states[0]]
        tail_spec = pl.BlockSpec((1, SUBLANES, tc), lambda i, c: (i, 0, c))
        tail_shape = jax.ShapeDtypeStruct((m // tm, SUBLANES, dff), F32)
    else:
        state_specs = [pl.BlockSpec((tm, tc), lambda i, c: (i, c))] * 2
        state_args = list(states)
        tail_spec = pl.BlockSpec((tm, tc), lambda i, c: (i, c))
        tail_shape = jax.ShapeDtypeStruct((m, dff), F32)
    est = (2 * (2 * tm * d * 4 + 3 * d * tc * 2 + 3 * tm * tc * 4) + (tm + halo) * d * 2 + tm * d * 4
           + 8 * (tm + halo) * tc * 4)
    return pl.pallas_call(
        kern,
        grid=(m // tm, nck),
        in_specs=[pl.BlockSpec((tm, d), lambda i, c: (i, 0))] + state_specs + [
            row(d), row(d),
            pl.BlockSpec((d, tc), lambda i, c: (0, c)),
            pl.BlockSpec((d, tc), lambda i, c: (0, nck + c)),
            pl.BlockSpec((SUBLANES, tc), lambda i, c: (0, c)),
            pl.BlockSpec((1, tc), lambda i, c: (0, c)),
            pl.BlockSpec((tc, d), lambda i, c: (c, 0)),
        ],
        out_specs=[pl.BlockSpec((tm, d), lambda i, c: (i, 0)), tail_spec],
        out_shape=[jax.ShapeDtypeStruct((m, d), F32), tail_shape],
        scratch_shapes=[pltpu.VMEM((tm + halo, d), BF16), pltpu.VMEM((tm, d), F32)],
        compiler_params=pltpu.CompilerParams(
            dimension_semantics=("arbitrary", "arbitrary"), vmem_limit_bytes=_vmem_limit(est)),
        name="ffn",
    )(x, *state_args, lnpre, lnpost, w_in, w_in, conv_w, conv_b, w_down)


def _pad_rows_front(a, rows):
    return jnp.pad(a, ((0, 0), (rows - a.shape[1], 0), (0, 0)))


def _pad_rows_back(a, rows):
    return jnp.pad(a, ((0, rows - a.shape[0]), (0, 0)))


def _lane_row(v):
    return jnp.pad(v.astype(F32), (0, LANES - v.shape[0]))[None, :]


def kernel(x_prompt, x_sample, cache_k, cache_v, page_table, state_gdn, state_gdn_conv, state_ffn_conv, meta_tokens, ln_mix_pre, ln_mix_post, ln_ffn_pre, ln_ffn_post, w_in, da_lambda_q1, da_lambda_k1, da_lambda_q2, da_lambda_k2, da_subln, gdn_conv_w, gdn_a_log, gdn_dt_bias, gdn_norm, w_branch_a, w_branch_b, w_out, w_ffn_in, ffn_conv_w, ffn_conv_b, w_down):
    bp, seq, d = x_prompt.shape
    ds, dseq, _ = x_sample.shape
    depth = w_in.shape[0]
    da_heads, da_hd = cache_k.shape[3], cache_k.shape[5]
    gdn_heads, gdn_dk, gdn_dv = state_gdn.shape[2:]
    assert da_heads == gdn_heads and 2 * da_hd == LANES and gdn_dk == LANES and gdn_dv == LANES
    nh = da_heads
    wq = nh * LANES
    dff = w_down.shape[1]
    n_meta = meta_tokens.shape[0]
    tp = n_meta + seq
    assert dseq == SUBLANES

    widths = (wq, wq, wq, 3 * wq, wq, nh, nh, d, d)
    cuts = [0]
    for wdt in widths:
        cuts.append(cuts[-1] + wdt)
    seg = lambda w, k: w[..., cuts[k]:cuts[k + 1]]
    gate_col = 6 * wq
    assert gate_col % d == 0
    offs_gdn = (3 * nh, 4 * nh, 5 * nh, (6 * wq + 2 * d) // LANES, (7 * wq + 2 * d) // LANES)
    w_in_r = jnp.concatenate(
        [seg(w_in, 0), seg(w_in, 1), seg(w_in, 2), seg(w_in, 3), seg(w_in, 7), seg(w_in, 8), seg(w_in, 4),
         seg(w_in, 5), seg(w_in, 6), jnp.zeros((depth, d, LANES - 2 * nh), w_in.dtype)], axis=-1).astype(BF16)
    n_in = w_in_r.shape[-1]
    tn = 1920
    assert n_in % tn == 0

    wa_b, wb_b, wo_b = w_branch_a.astype(BF16), w_branch_b.astype(BF16), w_out.astype(BF16)
    wffn_b, wdown_b = w_ffn_in.astype(BF16), w_down.astype(BF16)
    gconv_w = jnp.pad(gdn_conv_w, ((0, 0), (0, SUBLANES - GDN_CONV), (0, 0)))
    fconv_w = jnp.pad(ffn_conv_w, ((0, 0), (0, SUBLANES - FFN_CONV), (0, 0)))
    lqk = jnp.stack([da_lambda_q1, da_lambda_k1, da_lambda_q2, da_lambda_k2], axis=1).astype(F32)

    xp = jnp.concatenate([jnp.broadcast_to(meta_tokens[None], (bp, n_meta, d)), x_prompt], axis=1).reshape(bp * tp, d)
    xs = x_sample.reshape(ds * dseq, d)
    page = cache_k.shape[2]
    ck = jnp.transpose(cache_k, (0, 1, 3, 4, 5, 2)).reshape(cache_k.shape[:2] + (wq, page))
    cv = cache_v.reshape(cache_v.shape[:2] + (page * nh, 2 * da_hd))
    zero_gconv = jnp.zeros((bp, SUBLANES, 3 * wq), F32)
    zero_s = jnp.zeros((bp, nh, gdn_dk, gdn_dv), F32)
    zero_fconv = jnp.zeros((bp, SUBLANES, dff), F32)
    tm_p = tp // 2
    tm_merge = tp // 6
    assert tp % 6 == 0 and tm_merge % SUBLANES == 0

    outs = [[] for _ in range(10)]
    for l in range(depth):
        lam_init = 0.8 - 0.6 * math.exp(-0.3 * l)
        row = lambda v: v[l][None, :]
        alog, dtb = _lane_row(gdn_a_log[l]), _lane_row(gdn_dt_bias[l])

        def mixer(x, h3, oa, s_conv, s0, head_c, out_dtype, tm):
            bsz, t, _ = h3.shape
            ob, s_new = _gdn(h3, offs_gdn, gconv_w[l], s_conv, s0, alog, dtb, row(gdn_norm),
                             n_heads=nh, head_c=head_c, main_c=GDN_CHUNK, group=GDN_GROUP, out_dtype=out_dtype)
            hflat = h3.reshape(bsz * t, n_in)
            x = _merge(x, oa.reshape(bsz * t, wq), ob.reshape(bsz * t, wq), hflat, gate_col // d,
                       wa_b[l], wb_b[l], wo_b[l], row(ln_mix_post), tm)
            k_new = h3[:, :, wq:2 * wq].reshape(bsz, t, nh, 2, da_hd)
            v_new = h3[:, :, 2 * wq:3 * wq].reshape(bsz, t, nh, 2 * da_hd)
            c_new = h3[:, t - (GDN_CONV - 1):, 3 * wq:6 * wq]
            return x, k_new, v_new, c_new, s_new

        hp3 = _norm_matmul(xp, row(ln_mix_pre), w_in_r[l], tm_p, tn).reshape(bp, tp, n_in)
        oa = _attn_prompt(hp3, lqk[l], row(da_subln), n_heads=nh, lam_init=lam_init, tq=512)
        xp, k_new, v_new, c_new, s_new = mixer(xp, hp3, oa, zero_gconv, zero_s, n_meta, BF16, tm_merge)
        xp, tail = _ffn(xp, [zero_fconv], row(ln_ffn_pre), row(ln_ffn_post), wffn_b[l], fconv_w[l],
                        row(ffn_conv_b), wdown_b[l], tm=tm_p, tc=256, long_seq=True, seq=tp)
        f_new = tail.reshape(bp, tp // tm_p, SUBLANES, dff)[:, -1, SUBLANES - (FFN_CONV - 1):]
        for lst, val in zip(outs[0::2], (k_new, v_new, s_new, c_new, f_new)):
            lst.append(val)

        hs3 = _norm_matmul(xs, row(ln_mix_pre), w_in_r[l], ds * dseq, tn).reshape(ds, dseq, n_in)
        oa = _attn_sample(hs3, ck, cv, page_table, l, lqk[l], row(da_subln), n_heads=nh, lam_init=lam_init, pps=8)
        xs, k_new, v_new, c_new, s_new = mixer(
            xs, hs3, oa, _pad_rows_front(state_gdn_conv[l], SUBLANES), state_gdn[l], dseq, F32, ds * dseq)
        buf = state_ffn_conv[l]
        st1 = jnp.pad(buf[:, 1:2], ((0, 0), (0, dseq - 1), (0, 0))).reshape(ds * dseq, dff)
        st2 = jnp.pad(buf, ((0, 0), (0, dseq - 2), (0, 0))).reshape(ds * dseq, dff)
        xs, gate_rows = _ffn(xs, [st1, st2], row(ln_ffn_pre), row(ln_ffn_post), wffn_b[l], fconv_w[l],
                             row(ffn_conv_b), wdown_b[l], tm=ds * dseq, tc=256, long_seq=False, seq=dseq)
        f_new = gate_rows.reshape(ds, dseq, dff)[:, dseq - (FFN_CONV - 1):]
        for lst, val in zip(outs[1::2], (k_new, v_new, s_new, c_new, f_new)):
            lst.append(val)

    kp_l, ks_l, vp_l, vs_l, sp_l, ss_l, cp_l, cs_l, fp_l, fs_l = outs
    return (xp.reshape(bp, tp, d)[:, n_meta:], xs.reshape(ds, dseq, d),
            jnp.stack(kp_l), jnp.stack(vp_l), jnp.stack(ks_l), jnp.stack(vs_l),
            jnp.stack(sp_l), jnp.stack(ss_l), jnp.stack(cp_l), jnp.stack(cs_l),
            jnp.stack(fp_l), jnp.stack(fs_l))
```

```python
import functools
import math

import jax
import jax.numpy as jnp
from jax import lax
from jax.experimental import pallas as pl
from jax.experimental.pallas import tpu as pltpu

F32 = jnp.float32
BF16 = jnp.bfloat16
HI = lax.Precision.HIGHEST

RMS_EPS = 1e-6
L2_EPS = 1e-6
N_META = 16
GDN_CHUNK = 64
GDN_GROUP = 256
GDN_CONV = 4
FFN_CONV = 3
INV_BLOCK = 16

LANES = 128
SUBLANES = 8
VMEM_CAP = 56 << 20


def _vmem_limit(nbytes):
    return int(min(VMEM_CAP, nbytes + (16 << 20)))


def _rms_rows(x, w):
    ms = jnp.mean(x * x, axis=-1, keepdims=True)
    return x * lax.rsqrt(ms + RMS_EPS) * w


def _dot(a, b, precision=None):
    return jnp.dot(a, b, preferred_element_type=F32, precision=precision)


def _dot_nt(a, b, precision=None):
    return lax.dot_general(a, b, (((1,), (1,)), ((), ())), preferred_element_type=F32, precision=precision)


def _dot_tn(a, b, precision=None):
    return lax.dot_general(a, b, (((0,), (0,)), ((), ())), preferred_element_type=F32, precision=precision)


def _div_pow2(x, n):
    assert n & (n - 1) == 0
    if isinstance(x, int):
        return x // n
    return lax.shift_right_logical(x, jnp.asarray(n.bit_length() - 1, x.dtype))


def _mod_pow2(x, n):
    assert n & (n - 1) == 0
    if isinstance(x, int):
        return x % n
    return lax.bitwise_and(x, jnp.asarray(n - 1, x.dtype))


def _sigmoid(x):
    return 1.0 / (1.0 + jnp.exp(-x))


def _softplus(x):
    return jnp.maximum(x, 0.0) + jnp.log1p(jnp.exp(-jnp.abs(x)))


def _gelu_tanh(x):
    c = 0.7978845608028654
    return 0.5 * x * (1.0 + jnp.tanh(c * (x + 0.044715 * (x * x * x))))


def _norm_matmul_kernel(x_ref, lnw_ref, w_ref, o_ref, k_ref, v_ref, xn_ref, *, kcol, n_heads):
    @pl.when(pl.program_id(1) == 0)
    def _():
        xn_ref[...] = _rms_rows(x_ref[...], lnw_ref[...]).astype(BF16)

    res = _dot(xn_ref[...], w_ref[...])
    o_ref[...] = res

    @pl.when(pl.program_id(1) == 0)
    def _():
        tm, wk = k_ref.shape
        k_ref[...] = res[:, kcol:kcol + wk]
        for hh in range(n_heads):
            v_ref[pl.ds(hh, tm, stride=n_heads), :] = res[:, kcol + wk + hh * LANES:kcol + wk + (hh + 1) * LANES]


def _norm_matmul(x, lnw, w, tm, tn, *, kcol, n_heads):
    m, d = x.shape
    n = w.shape[1]
    wk = n_heads * LANES
    assert m % tm == 0 and n % tn == 0 and kcol + 2 * wk <= tn
    est = 2 * (tm * d * 4 + d * tn * 2 + tm * tn * 4 + 2 * tm * wk * 4) + tm * d * 2
    return pl.pallas_call(
        functools.partial(_norm_matmul_kernel, kcol=kcol, n_heads=n_heads),
        grid=(m // tm, n // tn),
        in_specs=[
            pl.BlockSpec((tm, d), lambda i, j: (i, 0)),
            pl.BlockSpec((1, d), lambda i, j: (0, 0)),
            pl.BlockSpec((d, tn), lambda i, j: (0, j)),
        ],
        out_specs=[pl.BlockSpec((tm, tn), lambda i, j: (i, j)),
                   pl.BlockSpec((tm, wk), lambda i, j: (i, 0)),
                   pl.BlockSpec((tm * n_heads, LANES), lambda i, j: (i, 0))],
        out_shape=[jax.ShapeDtypeStruct((m, n), F32),
                   jax.ShapeDtypeStruct((m, wk), F32),
                   jax.ShapeDtypeStruct((m * n_heads, LANES), F32)],
        scratch_shapes=[pltpu.VMEM((tm, d), BF16)],
        compiler_params=pltpu.CompilerParams(
            dimension_semantics=("arbitrary", "arbitrary"), vmem_limit_bytes=_vmem_limit(est)),
        name="in_proj",
    )(x, lnw, w)


def _diff_lambda(lqk, lam_init):
    return (jnp.exp(jnp.sum(lqk[0:1] * lqk[1:2], axis=-1, keepdims=True))
            - jnp.exp(jnp.sum(lqk[2:3] * lqk[3:4], axis=-1, keepdims=True)) + lam_init)


def _sub_norm(o, subw, lam_init):
    return _rms_rows(o, subw) * (1.0 - lam_init)


def _attn_prompt_kernel(lqk_ref, subw_ref, q_ref, k_ref, v_ref, o_ref, kb_ref, vb_ref, *, seq, tq, lam_init):
    dh = LANES // 2
    lam = _diff_lambda(lqk_ref[...], lam_init)
    kb_ref[...] = k_ref[0].astype(BF16)
    vb_ref[...] = v_ref[0].astype(BF16)
    lane = lax.broadcasted_iota(jnp.int32, (1, LANES), 1)
    n_main = seq // tq
    tail = seq - n_main * tq

    def lane_sum(p):
        part = p[:, 0:LANES]
        for c in range(1, p.shape[1] // LANES):
            part = part + p[:, c * LANES:(c + 1) * LANES]
        return jnp.sum(part, axis=-1, keepdims=True)

    def q_block(qs, nq, nk_full, ks_last, tk_last, key_lo):
        q = q_ref[0, qs:qs + nq, :] * (dh ** -0.5)
        row = qs + lax.broadcasted_iota(jnp.int32, (nq, 1), 0)
        col = ks_last + lax.broadcasted_iota(jnp.int32, (1, tk_last), 1)
        keep = col <= row
        if key_lo is not None:
            keep = jnp.logical_and(keep, col >= key_lo)
        k_last, v_last = kb_ref[ks_last:ks_last + tk_last, :], vb_ref[ks_last:ks_last + tk_last, :]
        outs = []
        for qm in (jnp.where(lane < dh, q, 0.0).astype(BF16), jnp.where(lane >= dh, q, 0.0).astype(BF16)):
            s_last = jnp.where(keep, _dot_nt(qm, k_last), -jnp.inf)
            m = jnp.max(s_last, axis=-1, keepdims=True)
            if nk_full:
                s_full = _dot_nt(qm, kb_ref[0:nk_full, :])
                m = jnp.maximum(m, jnp.max(s_full, axis=-1, keepdims=True))
            p = jnp.exp(s_last - m)
            l = lane_sum(p)
            acc = _dot(p.astype(BF16), v_last)
            if nk_full:
                p = jnp.exp(s_full - m)
                l = l + lane_sum(p)
                acc = acc + _dot(p.astype(BF16), vb_ref[0:nk_full, :])
            outs.append(acc / l)
        o = outs[0] - lam * outs[1]
        o_ref[0, qs:qs + nq, :] = _sub_norm(o, subw_ref[...], lam_init).astype(o_ref.dtype)

    for qi in range(n_main):
        q_block(qi * tq, tq, qi * tq, qi * tq, tq, None)
    if tail:
        q_block(n_main * tq, tail, n_main * tq, seq - LANES, LANES, n_main * tq)


def _attn_prompt(h3, lqk, subw, *, n_heads, lam_init, tq):
    bsz, seq, _ = h3.shape
    assert seq >= LANES and (seq - (seq // tq) * tq) % 16 == 0
    kern = functools.partial(_attn_prompt_kernel, seq=seq, tq=tq, lam_init=lam_init)
    blk = lambda off: pl.BlockSpec((1, seq, LANES), lambda b, h: (b, 0, off + h))
    est = 2 * (3 * seq * LANES * 4 + seq * LANES * 2) + 2 * seq * LANES * 2 + 4 * tq * seq * 4
    return pl.pallas_call(
        kern,
        grid=(bsz, n_heads),
        in_specs=[
            pl.BlockSpec(lqk.shape, lambda b, h: (0, 0)),
            pl.BlockSpec((1, LANES), lambda b, h: (0, 0)),
            blk(0), blk(n_heads), blk(2 * n_heads),
        ],
        out_specs=pl.BlockSpec((1, seq, LANES), lambda b, h: (b, 0, h)),
        out_shape=jax.ShapeDtypeStruct((bsz, seq, n_heads * LANES), BF16),
        scratch_shapes=[pltpu.VMEM((seq, LANES), BF16), pltpu.VMEM((seq, LANES), BF16)],
        compiler_params=pltpu.CompilerParams(
            dimension_semantics=("arbitrary", "arbitrary"), vmem_limit_bytes=_vmem_limit(est)),
        name="attn_prompt",
    )(lqk, subw, h3, h3, h3)


def _attn_sample_kernel(pt_ref, lqk_ref, subw_ref, q_ref, kn_ref, vn_ref, *rest, pps, n_heads, lam_init):
    del pt_ref
    kp_refs, vp_refs = rest[:pps], rest[pps:2 * pps]
    o_ref, qrows_ref, m_ref, l_ref, acc_ref = rest[2 * pps:]
    dec_seq, width = q_ref.shape[1], q_ref.shape[2]
    dh = LANES // 2
    n_rows = 2 * n_heads * dec_seq
    page = kp_refs[0].shape[1]
    j = pl.program_id(1)

    @pl.when(j == 0)
    def _():
        q = q_ref[0] * (dh ** -0.5)
        qt = jnp.concatenate([q] * (2 * n_heads), axis=0)
        rgrp = _div_pow2(lax.broadcasted_iota(jnp.int32, (n_rows, width), 0), dec_seq)
        cgrp = _div_pow2(lax.broadcasted_iota(jnp.int32, (n_rows, width), 1), dh)
        qrows_ref[...] = jnp.where(rgrp == cgrp, qt, 0.0).astype(BF16)
        m_ref[...] = jnp.full(m_ref.shape, -jnp.inf, F32)
        l_ref[...] = jnp.zeros(l_ref.shape, F32)
        acc_ref[...] = jnp.zeros(acc_ref.shape, F32)

    qr = qrows_ref[...]

    def update(s_list, v_list):
        smax = s_list[0]
        for s in s_list[1:]:
            smax = jnp.maximum(smax, s)
        m_old = m_ref[...]
        m_new = jnp.maximum(m_old, jnp.max(smax, axis=-1, keepdims=True))
        alpha = jnp.exp(m_old - m_new)
        l = alpha * l_ref[...]
        acc = alpha * acc_ref[...]
        for s, v in zip(s_list, v_list):
            p = jnp.exp(s - m_new)
            l = l + p
            acc = acc + _dot(p.astype(BF16), v)
        m_ref[...] = m_new
        l_ref[...] = l
        acc_ref[...] = acc

    def value_page(vp):
        return jnp.concatenate(
            [vp[pl.ds(hh, page, stride=n_heads), :] for hh in range(n_heads)], axis=1).astype(BF16)

    update([_dot(qr, kp[...].astype(BF16)) for kp in kp_refs], [value_page(vp) for vp in vp_refs])

    @pl.when(j == pl.num_programs(1) - 1)
    def _():
        pad = jnp.zeros((page - dec_seq, width), F32)
        kn = jnp.concatenate([kn_ref[0], pad], axis=0).astype(BF16)
        vn = jnp.concatenate([vn_ref[0], pad], axis=0).astype(BF16)
        qpos = _mod_pow2(lax.broadcasted_iota(jnp.int32, (n_rows, 1), 0), dec_seq)
        col = lax.broadcasted_iota(jnp.int32, (1, page), 1)
        s = jnp.where(col <= qpos, _dot_nt(qr, kn), -jnp.inf)
        update([s], [vn])
        lam = _diff_lambda(lqk_ref[...], lam_init)
        inv_l = 1.0 / jnp.sum(l_ref[...], axis=-1, keepdims=True)
        acc = acc_ref[...] * inv_l
        for h in range(n_heads):
            r0 = 2 * h * dec_seq
            cols = slice(h * LANES, (h + 1) * LANES)
            o = acc[r0:r0 + dec_seq, cols] - lam * acc[r0 + dec_seq:r0 + 2 * dec_seq, cols]
            o_ref[0, :, cols] = _sub_norm(o, subw_ref[...], lam_init).astype(o_ref.dtype)


def _attn_sample(h3, cache_kt, cache_v, page_table, layer, lqk, subw, *, n_heads, lam_init, pps):
    bsz, dec_seq, _ = h3.shape
    n_pages = page_table.shape[1]
    width, page = cache_kt.shape[2], cache_kt.shape[3]
    assert n_pages % pps == 0 and width == n_heads * LANES and cache_v.shape[2:] == (page * n_heads, LANES)
    kern = functools.partial(_attn_sample_kernel, pps=pps, n_heads=n_heads, lam_init=lam_init)
    n_rows = 2 * n_heads * dec_seq

    def page_spec(i):
        return pl.BlockSpec((None, None, width, page), lambda b, j, pt: (layer, pt[b, j * pps + i], 0, 0))

    new_spec = lambda off: pl.BlockSpec((1, dec_seq, width), lambda b, j, pt: (b, 0, off))
    est = 2 * (2 * pps * page * width * 4) + 4 * n_rows * width * 4
    grid_spec = pltpu.PrefetchScalarGridSpec(
        num_scalar_prefetch=1,
        grid=(bsz, n_pages // pps),
        in_specs=[
            pl.BlockSpec(lqk.shape, lambda b, j, pt: (0, 0)),
            pl.BlockSpec((1, LANES), lambda b, j, pt: (0, 0)),
            new_spec(0), new_spec(1), new_spec(2),
        ] + [page_spec(i) for i in range(pps)] + [page_spec(i) for i in range(pps)],
        out_specs=pl.BlockSpec((1, dec_seq, width), lambda b, j, pt: (b, 0, 0)),
        scratch_shapes=[
            pltpu.VMEM((n_rows, width), BF16),
            pltpu.VMEM((n_rows, 1), F32),
            pltpu.VMEM((n_rows, LANES), F32),
            pltpu.VMEM((n_rows, width), F32),
        ],
    )
    return pl.pallas_call(
        kern,
        grid_spec=grid_spec,
        out_shape=jax.ShapeDtypeStruct((bsz, dec_seq, width), F32),
        compiler_params=pltpu.CompilerParams(
            dimension_semantics=("arbitrary", "arbitrary"), vmem_limit_bytes=_vmem_limit(est)),
        name="attn_sample",
    )(page_table, lqk, subw, h3, h3, h3, *([cache_kt] * pps), *([cache_v] * pps))


def _mm(a, b):
    return _dot(a.astype(BF16), b.astype(BF16))


def _mm_nt(a, b):
    return _dot_nt(a.astype(BF16), b.astype(BF16))


def _each(f, *streams):
    return [f(*args) for args in zip(*streams)]


def _nilpotent_inverse(lms, order, eye):
    ts = [eye - lm for lm in lms]
    ps = lms
    k = 2
    while k < order:
        ps = _each(_mm, ps, ps)
        ts = _each(lambda t, p: _mm(t, eye + p), ts, ps)
        k *= 2
    return ts


def _unit_lower_inverse(lms, ri, ci, c):
    eye = jnp.where(ri == ci, 1.0, 0.0)
    if c <= INV_BLOCK:
        return _nilpotent_inverse(lms, c, eye)
    same = _div_pow2(ri, INV_BLOCK) == _div_pow2(ci, INV_BLOCK)
    diags = [jnp.where(same, lm, 0.0) for lm in lms]
    t_diags = _nilpotent_inverse(diags, INV_BLOCK, eye)
    ns = _each(lambda td, lm, dg: _mm(td, lm - dg), t_diags, lms, diags)
    return _each(_mm, _nilpotent_inverse(ns, c // INV_BLOCK, eye), t_diags)


def _conv_silu(win, w):
    taps = w.shape[0]
    acc = win[SUBLANES:] * w[taps - 1:taps]
    for s in range(1, taps):
        acc = acc + pltpu.roll(win, s, 0)[SUBLANES:] * w[taps - 1 - s:taps - s]
    return acc * _sigmoid(acc)


def _l2_rows(x):
    return x * lax.rsqrt(jnp.sum(x * x, axis=-1, keepdims=True) + L2_EPS)


def _gdn_kernel(alog_ref, dtb_ref, nw_ref, cwq_ref, cwk_ref, cwv_ref, csq_ref, csk_ref, csv_ref,
                xq_ref, xk_ref, xv_ref, z_ref, ba_ref, s0_ref, o_ref, sout_ref,
                winq_ref, wink_ref, winv_ref, lhs_ref, add_ref, o0_ref, dec_ref,
                *, seq, n_heads, hpb, head_c, main_c, group):
    dk = LANES
    lane = lax.broadcasted_iota(jnp.int32, (1, LANES), 1)
    pick = lambda x, idx: jnp.sum(jnp.where(lane == idx, x, 0.0), axis=-1, keepdims=True)
    heads = [pl.program_id(1) * hpb + hh for hh in range(hpb)]
    lanes_of = [slice(hh * LANES, (hh + 1) * LANES) for hh in range(hpb)]
    neg_a = [-jnp.exp(pick(alog_ref[...], h)) for h in heads]
    dtb = [pick(dtb_ref[...], h) for h in heads]
    for win, cs, x in ((winq_ref, csq_ref, xq_ref), (wink_ref, csk_ref, xk_ref), (winv_ref, csv_ref, xv_ref)):
        win[0:SUBLANES, :] = cs[0]
        win[SUBLANES:SUBLANES + seq, :] = x[0]

    def prep_all(slot, start, rows, c):
        n = rows // c
        conv = lambda win_ref, cw_ref: [
            _conv_silu(win_ref[pl.ds(start, rows + SUBLANES), ls], cw_ref[0:GDN_CONV, ls]) for ls in lanes_of]
        q, k, v = conv(winq_ref, cwq_ref), conv(wink_ref, cwk_ref), conv(winv_ref, cwv_ref)
        qn = [_l2_rows(x) * (dk ** -0.5) for x in q]
        kn = [_l2_rows(x) for x in k]
        ba = ba_ref[0, pl.ds(start, rows), :]
        beta = [_sigmoid(pick(ba, h)) for h in heads]
        g = [na * _softplus(pick(ba, h + n_heads) + db) for h, na, db in zip(heads, neg_a, dtb)]
        ri = lax.broadcasted_iota(jnp.int32, (rows, rows), 0)
        ci = lax.broadcasted_iota(jnp.int32, (rows, rows), 1)
        causal, strict, upper = ci <= ri, ci < ri, ri <= ci
        if n > 1:
            same = _div_pow2(ri, c) == _div_pow2(ci, c)
            causal, strict, upper = (jnp.logical_and(same, m) for m in (causal, strict, upper))
        tri = jnp.where(causal, 1.0, 0.0)
        gcb = [_dot(tri, jnp.broadcast_to(x, (rows, LANES)), HI) for x in g]
        gc = [x[:, 0:1] for x in gcb]
        if rows % LANES == 0:
            g_row = [jnp.transpose(x)[0:1, :] for x in gcb]
        else:
            ones = jnp.ones((rows, rows), F32)
            g_row = [_dot(ones, jnp.where(upper, x, 0.0), HI) for x in g]
        decay = _each(lambda a, b: jnp.exp(jnp.where(causal, a - b, -jnp.inf)), gc, g_row)
        chunk_of_row = _div_pow2(lax.broadcasted_iota(jnp.int32, (rows, 1), 0), c)

        def chunk_total(x):
            tot = x[c - 1:c, :]
            for cc in range(1, n):
                tot = jnp.where(chunk_of_row == cc, x[(cc + 1) * c - 1:(cc + 1) * c, :], tot)
            return tot

        gl = [chunk_total(x) for x in gc]
        kb = _each(lambda a, b: a * b, kn, beta)
        both = _each(lambda a, b, kk: _mm_nt(jnp.concatenate([a, b], axis=0), kk), kb, qn, kn)
        lm = _each(lambda x, dc: jnp.where(strict, x[:rows] * dc, 0.0), both, decay)
        qk = _each(lambda x, dc: x[rows:] * dc, both, decay)
        t_inv = _unit_lower_inverse(lm, ri, ci, c)
        uw = _each(lambda t, vv, bb, kbb, gg: _mm(t, jnp.concatenate([vv * bb, kbb * jnp.exp(gg)], axis=1)),
                   t_inv, v, beta, kb, gc)
        qkuw = _each(_mm, qk, uw)
        qeff = _each(lambda a, gg, x: a * jnp.exp(gg) - x[:, LANES:], qn, gc, qkuw)
        kd = _each(lambda kk, tot, gg: kk * jnp.exp(tot - gg), kn, gl, gc)
        egl = [jnp.exp(x) for x in gl]
        for hh in range(hpb):
            o0_ref[hh, slot, 0:rows, :] = qkuw[hh][:, :LANES]
        for cc in range(n):
            r = slice(cc * c, (cc + 1) * c)
            ab = _each(lambda a, b: _dot_tn(a[r].astype(BF16), b[r].astype(BF16)), kd, uw)
            for hh in range(hpb):
                lhs_ref[hh, slot, cc, 0:LANES, :] = -ab[hh][:, LANES:]
                lhs_ref[hh, slot, cc, LANES:LANES + c, :] = qeff[hh][r]
                add_ref[hh, slot, cc] = ab[hh][:, :LANES]
                dec_ref[hh, slot, cc] = jnp.broadcast_to(egl[hh][cc * c:cc * c + 1, :], (SUBLANES, LANES))

    def finish(slot, start, rows, c, states):
        states = list(states)
        for cc in range(rows // c):
            for hh in range(hpb):
                s, ls = states[hh], lanes_of[hh]
                r = _mm(lhs_ref[hh, slot, cc, 0:LANES + c, :], s)
                o = r[LANES:] + o0_ref[hh, slot, cc * c:(cc + 1) * c, :]
                states[hh] = s * dec_ref[hh, slot, cc, 0:1, :] + (r[:LANES] + add_ref[hh, slot, cc])
                z = z_ref[0, pl.ds(start + cc * c, c), ls]
                o_ref[0, pl.ds(start + cc * c, c), ls] = (
                    _rms_rows(o, nw_ref[...]) * (z * _sigmoid(z))).astype(o_ref.dtype)
        return tuple(states)

    prep_all(0, 0, head_c, head_c)
    states = finish(0, 0, head_c, head_c, tuple(s0_ref[0, hh] for hh in range(hpb)))
    n_groups = (seq - head_c) // group
    if n_groups:
        gstart = lambda gi: pl.multiple_of(head_c + gi * group, math.gcd(head_c, group))
        prep_all(0, head_c, group, main_c)

        def body(gi, states):
            states = finish(_mod_pow2(gi, 2), gstart(gi), group, main_c, states)
            prep_all(_mod_pow2(gi + 1, 2), gstart(gi + 1), group, main_c)
            return states

        states = lax.fori_loop(0, n_groups - 1, body, states)
        last = n_groups - 1
        states = finish(last % 2, head_c + last * group, group, main_c, states)
    for hh in range(hpb):
        sout_ref[0, hh] = states[hh]


def _gdn(h3, offs, conv_w, conv_state, s0, alog, dtb, norm_w, *, n_heads, hpb, head_c, main_c, group, out_dtype):
    bsz, seq, _ = h3.shape
    assert (seq - head_c) % group == 0 and group % main_c == 0 and n_heads % hpb == 0
    assert all(o % hpb == 0 for o in offs[:4])
    has_main = seq > head_c
    rows_max = group if has_main else head_c
    c_max = max(head_c, main_c) if has_main else head_c
    n_cc = max(rows_max // main_c, 1) if has_main else 1
    oq, ok, ov, oz, oba = offs
    wb = hpb * LANES
    kern = functools.partial(_gdn_kernel, seq=seq, n_heads=n_heads, hpb=hpb, head_c=head_c, main_c=main_c,
                             group=group)
    row = lambda: pl.BlockSpec((1, LANES), lambda b, h: (0, 0))
    cwspec = lambda off: pl.BlockSpec((SUBLANES, wb), lambda b, h: (0, off // hpb + h))
    csspec = lambda off: pl.BlockSpec((1, SUBLANES, wb), lambda b, h: (b, 0, off // hpb + h))
    xspec = lambda off: pl.BlockSpec((1, seq, wb), lambda b, h: (b, 0, off // hpb + h))
    sspec = pl.BlockSpec((1, hpb, LANES, LANES), lambda b, h: (b, h, 0, 0))
    est = 2 * (5 * seq * wb * 4 + seq * LANES * 4) + 3 * (seq + SUBLANES) * wb * 4
    return pl.pallas_call(
        kern,
        grid=(bsz, n_heads // hpb),
        in_specs=[row(), row(), row(),
                  cwspec(0), cwspec(n_heads), cwspec(2 * n_heads),
                  csspec(0), csspec(n_heads), csspec(2 * n_heads),
                  xspec(oq), xspec(ok), xspec(ov), xspec(oz),
                  pl.BlockSpec((1, seq, LANES), lambda b, h: (b, 0, oba)),
                  sspec],
        out_specs=[pl.BlockSpec((1, seq, wb), lambda b, h: (b, 0, h)), sspec],
        out_shape=[jax.ShapeDtypeStruct((bsz, seq, n_heads * LANES), out_dtype),
                   jax.ShapeDtypeStruct(s0.shape, F32)],
        scratch_shapes=[pltpu.VMEM((seq + SUBLANES, wb), F32)] * 3 + [
            pltpu.VMEM((hpb, 2, n_cc, LANES + c_max, LANES), F32),
            pltpu.VMEM((hpb, 2, n_cc, LANES, LANES), F32),
            pltpu.VMEM((hpb, 2, rows_max, LANES), F32),
            pltpu.VMEM((hpb, 2, n_cc, SUBLANES, LANES), F32),
        ],
        compiler_params=pltpu.CompilerParams(
            dimension_semantics=("arbitrary", "arbitrary"), vmem_limit_bytes=_vmem_limit(est)),
        name="gdn",
    )(alog, dtb, norm_w, conv_w, conv_w, conv_w, conv_state, conv_state, conv_state,
      h3, h3, h3, h3, h3, s0)


def _merge_kernel(x_ref, oa_ref, ob_ref, ga_ref, gb_ref, wa_ref, wb_ref, wo_ref, lnw_ref, o_ref):
    ya = _dot(oa_ref[...].astype(BF16), wa_ref[...])
    yb = _dot(ob_ref[...].astype(BF16), wb_ref[...])
    mix = _sigmoid(ga_ref[...]) * ya + _sigmoid(gb_ref[...]) * yb
    y = _dot(mix.astype(BF16), wo_ref[...])
    o_ref[...] = x_ref[...] + _rms_rows(y, lnw_ref[...])


def _merge(x, oa, ob, hflat, gate_blk, wa, wb, wo, lnw, tm):
    m, d = x.shape
    wbr = oa.shape[1]
    assert m % tm == 0
    est = 2 * (4 * tm * d * 4 + 2 * tm * wbr * 4 + (2 * wbr * d + d * d) * 2) + 4 * tm * d * 4
    const = lambda shp: pl.BlockSpec(shp, lambda i: (0, 0), pipeline_mode=pl.Buffered(1))
    return pl.pallas_call(
        _merge_kernel,
        grid=(m // tm,),
        in_specs=[
            pl.BlockSpec((tm, d), lambda i: (i, 0)),
            pl.BlockSpec((tm, wbr), lambda i: (i, 0)),
            pl.BlockSpec((tm, wbr), lambda i: (i, 0)),
            pl.BlockSpec((tm, d), lambda i: (i, gate_blk)),
            pl.BlockSpec((tm, d), lambda i: (i, gate_blk + 1)),
            const(wa.shape), const(wb.shape), const(wo.shape), const((1, d)),
        ],
        out_specs=pl.BlockSpec((tm, d), lambda i: (i, 0)),
        out_shape=jax.ShapeDtypeStruct((m, d), F32),
        compiler_params=pltpu.CompilerParams(
            dimension_semantics=("arbitrary",), vmem_limit_bytes=_vmem_limit(est)),
        name="merge",
    )(x, oa, ob, hflat, hflat, wa, wb, wo, lnw)


def _ffn_kernel(*refs, long_seq, tiles_per_seq, tc):
    if long_seq:
        (x_ref, xprev_ref, st_ref, lnpre_ref, lnpost_ref, win_ref, cw_ref, cb_ref, wd_ref,
         o_ref, tail_ref, xe_ref, act_ref) = refs
    else:
        (x_ref, st1_ref, st2_ref, lnpre_ref, lnpost_ref, win_ref, cw_ref, cb_ref, wd_ref,
         o_ref, tail_ref, xe_ref, act_ref) = refs
    tm = x_ref.shape[0]
    dff = wd_ref.shape[0]
    xn = _rms_rows(x_ref[...], lnpre_ref[...])
    if long_seq:
        xn = jnp.concatenate([_rms_rows(xprev_ref[...], lnpre_ref[...]), xn], axis=0)
        first = (pl.program_id(0) % tiles_per_seq) == 0
    else:
        pos = _mod_pow2(lax.broadcasted_iota(jnp.int32, (tm, 1), 0), SUBLANES)
    xe_ref[...] = xn.astype(BF16)
    for c in range(dff // tc):
        cols = slice(c * tc, (c + 1) * tc)
        xe = xe_ref[...]
        up = _dot(xe, win_ref[:, cols])
        gate = _dot(xe, win_ref[:, dff + c * tc:dff + (c + 1) * tc])
        if long_seq:
            halo = jnp.where(first, st_ref[0, :, cols], gate[0:SUBLANES])
            gate = jnp.concatenate([halo, gate[SUBLANES:]], axis=0)
            g1 = pltpu.roll(gate, 1, 0)[SUBLANES:]
            g2 = pltpu.roll(gate, 2, 0)[SUBLANES:]
            g0 = gate[SUBLANES:]
            up = up[SUBLANES:]
            tail_ref[0, :, cols] = gate[tm:]
        else:
            g1 = jnp.where(pos >= 1, pltpu.roll(gate, 1, 0), st1_ref[:, cols])
            g2 = jnp.where(pos >= 2, pltpu.roll(gate, 2, 0), st2_ref[:, cols])
            g0 = gate
            tail_ref[:, cols] = gate
        y = cw_ref[0:1, cols] * g2 + cw_ref[1:2, cols] * g1 + cw_ref[2:3, cols] * g0 + cb_ref[:, cols]
        act_ref[:, cols] = (_gelu_tanh(y) * up).astype(BF16)
    o_ref[...] = x_ref[...] + _rms_rows(_dot(act_ref[...], wd_ref[...]), lnpost_ref[...])


def _ffn(x, states, lnpre, lnpost, w_in, conv_w, conv_b, w_down, *, tm, tc, long_seq, seq):
    m, d = x.shape
    dff = w_down.shape[0]
    assert m % tm == 0 and dff % tc == 0
    halo = SUBLANES if long_seq else 0
    tiles_per_seq = seq // tm if long_seq else 1
    kern = functools.partial(_ffn_kernel, long_seq=long_seq, tiles_per_seq=tiles_per_seq, tc=tc)
    const = lambda shp: pl.BlockSpec(shp, lambda i: (0,) * len(shp), pipeline_mode=pl.Buffered(1))
    if long_seq:
        assert seq % tm == 0 and tm % SUBLANES == 0
        bpt = tm // SUBLANES
        state_specs = [
            pl.BlockSpec((SUBLANES, d), lambda i: (jnp.maximum(i * bpt - 1, 0), 0)),
            pl.BlockSpec((1, SUBLANES, dff), lambda i: (i // tiles_per_seq, 0, 0)),
        ]
        state_args = [x, states[0]]
        tail_spec = pl.BlockSpec((1, SUBLANES, dff), lambda i: (i, 0, 0))
        tail_shape = jax.ShapeDtypeStruct((m // tm, SUBLANES, dff), F32)
        state_bytes = 2 * SUBLANES * dff * 4
    else:
        state_specs = [pl.BlockSpec((tm, dff), lambda i: (i, 0))] * 2
        state_args = list(states)
        tail_spec = pl.BlockSpec((tm, dff), lambda i: (i, 0))
        tail_shape = jax.ShapeDtypeStruct((m, dff), F32)
        state_bytes = 3 * tm * dff * 4
    est = (2 * (2 * tm * d * 4 + state_bytes) + 3 * d * dff * 2 + (tm + halo) * d * 2 + tm * dff * 2
           + 8 * (tm + halo) * tc * 4 + tm * d * 4)
    return pl.pallas_call(
        kern,
        grid=(m // tm,),
        in_specs=[pl.BlockSpec((tm, d), lambda i: (i, 0))] + state_specs + [
            const((1, d)), const((1, d)), const(w_in.shape), const(conv_w.shape), const(conv_b.shape),
            const(w_down.shape),
        ],
        out_specs=[pl.BlockSpec((tm, d), lambda i: (i, 0)), tail_spec],
        out_shape=[jax.ShapeDtypeStruct((m, d), F32), tail_shape],
        scratch_shapes=[pltpu.VMEM((tm + halo, d), BF16), pltpu.VMEM((tm, dff), BF16)],
        compiler_params=pltpu.CompilerParams(
            dimension_semantics=("arbitrary",), vmem_limit_bytes=_vmem_limit(est)),
        name="ffn",
    )(x, *state_args, lnpre, lnpost, w_in, conv_w, conv_b, w_down)


def _pad_rows_front(a, rows):
    return jnp.pad(a, ((0, 0), (rows - a.shape[1], 0), (0, 0)))


def _pad_rows_back(a, rows):
    return jnp.pad(a, ((0, rows - a.shape[0]), (0, 0)))


def _lane_row(v):
    return jnp.pad(v.astype(F32), (0, LANES - v.shape[0]))[None, :]


def kernel(x_prompt, x_sample, cache_k, cache_v, page_table, state_gdn, state_gdn_conv, state_ffn_conv, meta_tokens, ln_mix_pre, ln_mix_post, ln_ffn_pre, ln_ffn_post, w_in, da_lambda_q1, da_lambda_k1, da_lambda_q2, da_lambda_k2, da_subln, gdn_conv_w, gdn_a_log, gdn_dt_bias, gdn_norm, w_branch_a, w_branch_b, w_out, w_ffn_in, ffn_conv_w, ffn_conv_b, w_down):
    bp, seq, d = x_prompt.shape
    ds, dseq, _ = x_sample.shape
    depth = w_in.shape[0]
    da_heads, da_hd = cache_k.shape[3], cache_k.shape[5]
    gdn_heads, gdn_dk, gdn_dv = state_gdn.shape[2:]
    assert da_heads == gdn_heads and 2 * da_hd == LANES and gdn_dk == LANES and gdn_dv == LANES
    nh = da_heads
    wq = nh * LANES
    dff = w_down.shape[1]
    n_meta = meta_tokens.shape[0]
    tp = n_meta + seq
    assert dseq == SUBLANES

    widths = (wq, wq, wq, 3 * wq, wq, nh, nh, d, d)
    cuts = [0]
    for wdt in widths:
        cuts.append(cuts[-1] + wdt)
    seg = lambda w, k: w[..., cuts[k]:cuts[k + 1]]
    gate_col = 6 * wq
    assert gate_col % d == 0
    offs_gdn = (3 * nh, 4 * nh, 5 * nh, (6 * wq + 2 * d) // LANES, (7 * wq + 2 * d) // LANES)
    w_in_r = jnp.concatenate(
        [seg(w_in, 0), seg(w_in, 1), seg(w_in, 2), seg(w_in, 3), seg(w_in, 7), seg(w_in, 8), seg(w_in, 4),
         seg(w_in, 5), seg(w_in, 6), jnp.zeros((depth, d, LANES - 2 * nh), w_in.dtype)], axis=-1).astype(BF16)
    n_in = w_in_r.shape[-1]
    tn = 1920
    assert n_in % tn == 0

    wa_b, wb_b, wo_b = w_branch_a.astype(BF16), w_branch_b.astype(BF16), w_out.astype(BF16)
    wffn_b, wdown_b = w_ffn_in.astype(BF16), w_down.astype(BF16)
    gconv_w = jnp.pad(gdn_conv_w, ((0, 0), (0, SUBLANES - GDN_CONV), (0, 0)))
    fconv_w = jnp.pad(ffn_conv_w, ((0, 0), (0, SUBLANES - FFN_CONV), (0, 0)))
    lqk = jnp.stack([da_lambda_q1, da_lambda_k1, da_lambda_q2, da_lambda_k2], axis=1).astype(F32)

    xp = jnp.concatenate([jnp.broadcast_to(meta_tokens[None], (bp, n_meta, d)), x_prompt], axis=1).reshape(bp * tp, d)
    xs = x_sample.reshape(ds * dseq, d)
    page = cache_k.shape[2]
    ck = jnp.transpose(cache_k, (0, 1, 3, 4, 5, 2)).reshape(cache_k.shape[:2] + (wq, page))
    cv = cache_v.reshape(cache_v.shape[:2] + (page * nh, 2 * da_hd))
    zero_gconv = jnp.zeros((bp, SUBLANES, 3 * wq), F32)
    zero_s = jnp.zeros((bp, nh, gdn_dk, gdn_dv), F32)
    zero_fconv = jnp.zeros((bp, SUBLANES, dff), F32)
    tm_p = tp // 2
    tm_ffn = tp // 3
    assert tp % 3 == 0 and tm_ffn % SUBLANES == 0
    tm_merge = tp // 3

    outs = [[] for _ in range(10)]
    for l in range(depth):
        lam_init = 0.8 - 0.6 * math.exp(-0.3 * l)
        row = lambda v: v[l][None, :]
        alog, dtb = _lane_row(gdn_a_log[l]), _lane_row(gdn_dt_bias[l])

        def mixer(x, h3, oa, s_conv, s0, head_c, hpb, out_dtype, tm):
            bsz, t, _ = h3.shape
            ob, s_new = _gdn(h3, offs_gdn, gconv_w[l], s_conv, s0, alog, dtb, row(gdn_norm), n_heads=nh, hpb=hpb,
                             head_c=head_c, main_c=GDN_CHUNK, group=GDN_GROUP, out_dtype=out_dtype)
            hflat = h3.reshape(bsz * t, n_in)
            x = _merge(x, oa.reshape(bsz * t, wq), ob.reshape(bsz * t, wq), hflat, gate_col // d,
                       wa_b[l], wb_b[l], wo_b[l], row(ln_mix_post), tm)
            c_new = h3[:, t - (GDN_CONV - 1):, 3 * wq:6 * wq]
            return x, c_new, s_new

        def in_proj(x, bsz, t, tm):
            h, k_rows, v_rows = _norm_matmul(x, row(ln_mix_pre), w_in_r[l], tm, tn, kcol=wq, n_heads=nh)
            return (h.reshape(bsz, t, n_in), k_rows.reshape(bsz, t, nh, 2, da_hd),
                    v_rows.reshape(bsz, t, nh, 2 * da_hd))

        hp3, k_new, v_new = in_proj(xp, bp, tp, tm_p)
        oa = _attn_prompt(hp3, lqk[l], row(da_subln), n_heads=nh, lam_init=lam_init, tq=512)
        xp, c_new, s_new = mixer(xp, hp3, oa, zero_gconv, zero_s, n_meta, 2, BF16, tm_merge)
        xp, tail = _ffn(xp, [zero_fconv], row(ln_ffn_pre), row(ln_ffn_post), wffn_b[l], fconv_w[l],
                        row(ffn_conv_b), wdown_b[l], tm=tm_ffn, tc=256, long_seq=True, seq=tp)
        f_new = tail.reshape(bp, tp // tm_ffn, SUBLANES, dff)[:, -1, SUBLANES - (FFN_CONV - 1):]
        for lst, val in zip(outs[0::2], (k_new, v_new, s_new, c_new, f_new)):
            lst.append(val)

        hs3, k_new, v_new = in_proj(xs, ds, dseq, ds * dseq)
        oa = _attn_sample(hs3, ck, cv, page_table, l, lqk[l], row(da_subln), n_heads=nh, lam_init=lam_init, pps=16)
        xs, c_new, s_new = mixer(
            xs, hs3, oa, _pad_rows_front(state_gdn_conv[l], SUBLANES), state_gdn[l], dseq, nh, F32, ds * dseq)
        buf = state_ffn_conv[l]
        st1 = jnp.pad(buf[:, 1:2], ((0, 0), (0, dseq - 1), (0, 0))).reshape(ds * dseq, dff)
        st2 = jnp.pad(buf, ((0, 0), (0, dseq - 2), (0, 0))).reshape(ds * dseq, dff)
        xs, gate_rows = _ffn(xs, [st1, st2], row(ln_ffn_pre), row(ln_ffn_post), wffn_b[l], fconv_w[l],
                             row(ffn_conv_b), wdown_b[l], tm=ds * dseq, tc=256, long_seq=False, seq=dseq)
        f_new = gate_rows.reshape(ds, dseq, dff)[:, dseq - (FFN_CONV - 1):]
        for lst, val in zip(outs[1::2], (k_new, v_new, s_new, c_new, f_new)):
            lst.append(val)

    kp_l, ks_l, vp_l, vs_l, sp_l, ss_l, cp_l, cs_l, fp_l, fs_l = outs
    return (xp.reshape(bp, tp, d)[:, n_meta:], xs.reshape(ds, dseq, d),
            jnp.stack(kp_l), jnp.stack(vp_l), jnp.stack(ks_l), jnp.stack(vs_l),
            jnp.stack(sp_l), jnp.stack(ss_l), jnp.stack(cp_l), jnp.stack(cs_l),
            jnp.stack(fp_l), jnp.stack(fs_l))
```

```python
import functools
import math

import jax
import jax.numpy as jnp
from jax import lax
from jax.experimental import pallas as pl
from jax.experimental.pallas import tpu as pltpu

F32 = jnp.float32
BF16 = jnp.bfloat16
HI = lax.Precision.HIGHEST

RMS_EPS = 1e-6
L2_EPS = 1e-6
N_META = 16
GDN_CHUNK = 64
GDN_GROUP = 256
GDN_CONV = 4
FFN_CONV = 3
INV_BLOCK = 16

LANES = 128
SUBLANES = 8
VMEM_CAP = 56 << 20


def _vmem_limit(nbytes):
    return int(min(VMEM_CAP, nbytes + (16 << 20)))


def _rms_rows(x, w):
    ms = jnp.mean(x * x, axis=-1, keepdims=True)
    return x * lax.rsqrt(ms + RMS_EPS) * w


def _dot(a, b, precision=None):
    return jnp.dot(a, b, preferred_element_type=F32, precision=precision)


def _dot_nt(a, b, precision=None):
    return lax.dot_general(a, b, (((1,), (1,)), ((), ())), preferred_element_type=F32, precision=precision)


def _dot_tn(a, b, precision=None):
    return lax.dot_general(a, b, (((0,), (0,)), ((), ())), preferred_element_type=F32, precision=precision)


def _div_pow2(x, n):
    assert n & (n - 1) == 0
    if isinstance(x, int):
        return x // n
    return lax.shift_right_logical(x, jnp.asarray(n.bit_length() - 1, x.dtype))


def _mod_pow2(x, n):
    assert n & (n - 1) == 0
    if isinstance(x, int):
        return x % n
    return lax.bitwise_and(x, jnp.asarray(n - 1, x.dtype))


def _sigmoid(x):
    return 1.0 / (1.0 + jnp.exp(-x))


def _softplus(x):
    return jnp.maximum(x, 0.0) + jnp.log1p(jnp.exp(-jnp.abs(x)))


def _gelu_tanh(x):
    c = 0.7978845608028654
    return 0.5 * x * (1.0 + jnp.tanh(c * (x + 0.044715 * (x * x * x))))


def _norm_matmul_kernel(x_ref, lnw_ref, w_ref, o_ref, k_ref, v_ref, xn_ref, *, kcol, n_heads):
    @pl.when(pl.program_id(1) == 0)
    def _():
        xn_ref[...] = _rms_rows(x_ref[...], lnw_ref[...]).astype(BF16)

    res = _dot(xn_ref[...], w_ref[...])
    o_ref[...] = res

    @pl.when(pl.program_id(1) == 0)
    def _():
        tm, wk = k_ref.shape
        k_ref[...] = res[:, kcol:kcol + wk]
        for hh in range(n_heads):
            v_ref[pl.ds(hh, tm, stride=n_heads), :] = res[:, kcol + wk + hh * LANES:kcol + wk + (hh + 1) * LANES]


def _norm_matmul(x, lnw, w, tm, tn, *, kcol, n_heads):
    m, d = x.shape
    n = w.shape[1]
    wk = n_heads * LANES
    assert m % tm == 0 and n % tn == 0 and kcol + 2 * wk <= tn
    est = 2 * (tm * d * 4 + d * tn * 2 + tm * tn * 4 + 2 * tm * wk * 4) + tm * d * 2
    return pl.pallas_call(
        functools.partial(_norm_matmul_kernel, kcol=kcol, n_heads=n_heads),
        grid=(m // tm, n // tn),
        in_specs=[
            pl.BlockSpec((tm, d), lambda i, j: (i, 0)),
            pl.BlockSpec((1, d), lambda i, j: (0, 0)),
            pl.BlockSpec((d, tn), lambda i, j: (0, j)),
        ],
        out_specs=[pl.BlockSpec((tm, tn), lambda i, j: (i, j)),
                   pl.BlockSpec((tm, wk), lambda i, j: (i, 0)),
                   pl.BlockSpec((tm * n_heads, LANES), lambda i, j: (i, 0))],
        out_shape=[jax.ShapeDtypeStruct((m, n), F32),
                   jax.ShapeDtypeStruct((m, wk), F32),
                   jax.ShapeDtypeStruct((m * n_heads, LANES), F32)],
        scratch_shapes=[pltpu.VMEM((tm, d), BF16)],
        compiler_params=pltpu.CompilerParams(
            dimension_semantics=("arbitrary", "arbitrary"), vmem_limit_bytes=_vmem_limit(est)),
        name="in_proj",
    )(x, lnw, w)


def _diff_lambda(lqk, lam_init):
    return (jnp.exp(jnp.sum(lqk[0:1] * lqk[1:2], axis=-1, keepdims=True))
            - jnp.exp(jnp.sum(lqk[2:3] * lqk[3:4], axis=-1, keepdims=True)) + lam_init)


def _sub_norm(o, subw, lam_init):
    return _rms_rows(o, subw) * (1.0 - lam_init)


def _attn_prompt_kernel(lqk_ref, subw_ref, q_ref, k_ref, v_ref, o_ref, kb_ref, vb_ref, *, seq, tq, lam_init):
    dh = LANES // 2
    lam = _diff_lambda(lqk_ref[...], lam_init)
    kb_ref[...] = k_ref[0].astype(BF16)
    vb_ref[...] = v_ref[0].astype(BF16)
    lane = lax.broadcasted_iota(jnp.int32, (1, LANES), 1)
    n_main = seq // tq
    tail = seq - n_main * tq

    def lane_sum(p):
        part = p[:, 0:LANES]
        for c in range(1, p.shape[1] // LANES):
            part = part + p[:, c * LANES:(c + 1) * LANES]
        return jnp.sum(part, axis=-1, keepdims=True)

    def q_block(qs, nq, nk_full, ks_last, tk_last, key_lo):
        q = q_ref[0, qs:qs + nq, :] * (dh ** -0.5)
        row = qs + lax.broadcasted_iota(jnp.int32, (nq, 1), 0)
        col = ks_last + lax.broadcasted_iota(jnp.int32, (1, tk_last), 1)
        keep = col <= row
        if key_lo is not None:
            keep = jnp.logical_and(keep, col >= key_lo)
        k_last, v_last = kb_ref[ks_last:ks_last + tk_last, :], vb_ref[ks_last:ks_last + tk_last, :]
        outs = []
        for qm in (jnp.where(lane < dh, q, 0.0).astype(BF16), jnp.where(lane >= dh, q, 0.0).astype(BF16)):
            s_last = jnp.where(keep, _dot_nt(qm, k_last), -jnp.inf)
            m = jnp.max(s_last, axis=-1, keepdims=True)
            if nk_full:
                s_full = _dot_nt(qm, kb_ref[0:nk_full, :])
                m = jnp.maximum(m, jnp.max(s_full, axis=-1, keepdims=True))
            p = jnp.exp(s_last - m)
            l = lane_sum(p)
            acc = _dot(p.astype(BF16), v_last)
            if nk_full:
                p = jnp.exp(s_full - m)
                l = l + lane_sum(p)
                acc = acc + _dot(p.astype(BF16), vb_ref[0:nk_full, :])
            outs.append(acc / l)
        o = outs[0] - lam * outs[1]
        o_ref[0, qs:qs + nq, :] = _sub_norm(o, subw_ref[...], lam_init).astype(o_ref.dtype)

    for qi in range(n_main):
        q_block(qi * tq, tq, qi * tq, qi * tq, tq, None)
    if tail:
        q_block(n_main * tq, tail, n_main * tq, seq - LANES, LANES, n_main * tq)


def _attn_prompt(h3, lqk, subw, *, n_heads, lam_init, tq):
    bsz, seq, _ = h3.shape
    assert seq >= LANES and (seq - (seq // tq) * tq) % 16 == 0
    kern = functools.partial(_attn_prompt_kernel, seq=seq, tq=tq, lam_init=lam_init)
    blk = lambda off: pl.BlockSpec((1, seq, LANES), lambda b, h: (b, 0, off + h))
    est = 2 * (3 * seq * LANES * 4 + seq * LANES * 2) + 2 * seq * LANES * 2 + 4 * tq * seq * 4
    return pl.pallas_call(
        kern,
        grid=(bsz, n_heads),
        in_specs=[
            pl.BlockSpec(lqk.shape, lambda b, h: (0, 0)),
            pl.BlockSpec((1, LANES), lambda b, h: (0, 0)),
            blk(0), blk(n_heads), blk(2 * n_heads),
        ],
        out_specs=pl.BlockSpec((1, seq, LANES), lambda b, h: (b, 0, h)),
        out_shape=jax.ShapeDtypeStruct((bsz, seq, n_heads * LANES), BF16),
        scratch_shapes=[pltpu.VMEM((seq, LANES), BF16), pltpu.VMEM((seq, LANES), BF16)],
        compiler_params=pltpu.CompilerParams(
            dimension_semantics=("arbitrary", "arbitrary"), vmem_limit_bytes=_vmem_limit(est)),
        name="attn_prompt",
    )(lqk, subw, h3, h3, h3)


def _attn_sample_kernel(pt_ref, lqk_ref, subw_ref, q_ref, kn_ref, vn_ref, *rest, pps, n_heads, lam_init):
    del pt_ref
    kp_refs, vp_refs = rest[:pps], rest[pps:2 * pps]
    o_ref, qrows_ref, m_ref, l_ref, acc_ref = rest[2 * pps:]
    dec_seq, width = q_ref.shape[1], q_ref.shape[2]
    dh = LANES // 2
    n_rows = 2 * n_heads * dec_seq
    page = kp_refs[0].shape[1]
    j = pl.program_id(1)

    @pl.when(j == 0)
    def _():
        q = q_ref[0] * (dh ** -0.5)
        qt = jnp.concatenate([q] * (2 * n_heads), axis=0)
        rgrp = _div_pow2(lax.broadcasted_iota(jnp.int32, (n_rows, width), 0), dec_seq)
        cgrp = _div_pow2(lax.broadcasted_iota(jnp.int32, (n_rows, width), 1), dh)
        qrows_ref[...] = jnp.where(rgrp == cgrp, qt, 0.0).astype(BF16)
        m_ref[...] = jnp.full(m_ref.shape, -jnp.inf, F32)
        l_ref[...] = jnp.zeros(l_ref.shape, F32)
        acc_ref[...] = jnp.zeros(acc_ref.shape, F32)

    qr = qrows_ref[...]

    def update(s_list, v_list):
        smax = s_list[0]
        for s in s_list[1:]:
            smax = jnp.maximum(smax, s)
        m_old = m_ref[...]
        m_new = jnp.maximum(m_old, jnp.max(smax, axis=-1, keepdims=True))
        alpha = jnp.exp(m_old - m_new)
        l = alpha * l_ref[...]
        acc = alpha * acc_ref[...]
        for s, v in zip(s_list, v_list):
            p = jnp.exp(s - m_new)
            l = l + p
            acc = acc + _dot(p.astype(BF16), v)
        m_ref[...] = m_new
        l_ref[...] = l
        acc_ref[...] = acc

    def value_page(vp):
        return jnp.concatenate(
            [vp[pl.ds(hh, page, stride=n_heads), :] for hh in range(n_heads)], axis=1).astype(BF16)

    update([_dot(qr, kp[...].astype(BF16)) for kp in kp_refs], [value_page(vp) for vp in vp_refs])

    @pl.when(j == pl.num_programs(1) - 1)
    def _():
        pad = jnp.zeros((page - dec_seq, width), F32)
        kn = jnp.concatenate([kn_ref[0], pad], axis=0).astype(BF16)
        vn = jnp.concatenate([vn_ref[0], pad], axis=0).astype(BF16)
        qpos = _mod_pow2(lax.broadcasted_iota(jnp.int32, (n_rows, 1), 0), dec_seq)
        col = lax.broadcasted_iota(jnp.int32, (1, page), 1)
        s = jnp.where(col <= qpos, _dot_nt(qr, kn), -jnp.inf)
        update([s], [vn])
        lam = _diff_lambda(lqk_ref[...], lam_init)
        inv_l = 1.0 / jnp.sum(l_ref[...], axis=-1, keepdims=True)
        acc = acc_ref[...] * inv_l
        for h in range(n_heads):
            r0 = 2 * h * dec_seq
            cols = slice(h * LANES, (h + 1) * LANES)
            o = acc[r0:r0 + dec_seq, cols] - lam * acc[r0 + dec_seq:r0 + 2 * dec_seq, cols]
            o_ref[0, :, cols] = _sub_norm(o, subw_ref[...], lam_init).astype(o_ref.dtype)


def _attn_sample(h3, cache_kt, cache_v, page_table, layer, lqk, subw, *, n_heads, lam_init, pps):
    bsz, dec_seq, _ = h3.shape
    n_pages = page_table.shape[1]
    width, page = cache_kt.shape[2], cache_kt.shape[3]
    assert n_pages % pps == 0 and width == n_heads * LANES and cache_v.shape[2:] == (page * n_heads, LANES)
    kern = functools.partial(_attn_sample_kernel, pps=pps, n_heads=n_heads, lam_init=lam_init)
    n_rows = 2 * n_heads * dec_seq

    def page_spec(i):
        return pl.BlockSpec((None, None, width, page), lambda b, j, pt: (layer, pt[b, j * pps + i], 0, 0))

    new_spec = lambda off: pl.BlockSpec((1, dec_seq, width), lambda b, j, pt: (b, 0, off))
    est = 2 * (2 * pps * page * width * 4) + 4 * n_rows * width * 4
    grid_spec = pltpu.PrefetchScalarGridSpec(
        num_scalar_prefetch=1,
        grid=(bsz, n_pages // pps),
        in_specs=[
            pl.BlockSpec(lqk.shape, lambda b, j, pt: (0, 0)),
            pl.BlockSpec((1, LANES), lambda b, j, pt: (0, 0)),
            new_spec(0), new_spec(1), new_spec(2),
        ] + [page_spec(i) for i in range(pps)] + [page_spec(i) for i in range(pps)],
        out_specs=pl.BlockSpec((1, dec_seq, width), lambda b, j, pt: (b, 0, 0)),
        scratch_shapes=[
            pltpu.VMEM((n_rows, width), BF16),
            pltpu.VMEM((n_rows, 1), F32),
            pltpu.VMEM((n_rows, LANES), F32),
            pltpu.VMEM((n_rows, width), F32),
        ],
    )
    return pl.pallas_call(
        kern,
        grid_spec=grid_spec,
        out_shape=jax.ShapeDtypeStruct((bsz, dec_seq, width), F32),
        compiler_params=pltpu.CompilerParams(
            dimension_semantics=("arbitrary", "arbitrary"), vmem_limit_bytes=_vmem_limit(est)),
        name="attn_sample",
    )(page_table, lqk, subw, h3, h3, h3, *([cache_kt] * pps), *([cache_v] * pps))


def _mm(a, b):
    return _dot(a.astype(BF16), b.astype(BF16))


def _mm_nt(a, b):
    return _dot_nt(a.astype(BF16), b.astype(BF16))


def _each(f, *streams):
    return [f(*args) for args in zip(*streams)]


def _nilpotent_inverse(lms, order, eye):
    ts = [eye - lm for lm in lms]
    ps = [lm.astype(BF16) for lm in lms]
    k = 2
    while k < order:
        ps = [_dot(p, p).astype(BF16) for p in ps]
        ts = _each(lambda t, p: t + _dot(t.astype(BF16), p), ts, ps)
        k *= 2
    return ts


def _unit_lower_inverse(lms, ri, ci, c):
    eye = jnp.where(ri == ci, 1.0, 0.0)
    if c <= INV_BLOCK:
        return _nilpotent_inverse(lms, c, eye)
    same = _div_pow2(ri, INV_BLOCK) == _div_pow2(ci, INV_BLOCK)
    diags = [jnp.where(same, lm, 0.0) for lm in lms]
    t_diags = _nilpotent_inverse(diags, INV_BLOCK, eye)
    ns = _each(lambda td, lm, dg: _mm(td, lm - dg), t_diags, lms, diags)
    return _each(_mm, _nilpotent_inverse(ns, c // INV_BLOCK, eye), t_diags)


def _conv_silu(win, w):
    taps = w.shape[0]
    acc = win[SUBLANES:] * w[taps - 1:taps]
    for s in range(1, taps):
        acc = acc + pltpu.roll(win, s, 0)[SUBLANES:] * w[taps - 1 - s:taps - s]
    return acc * _sigmoid(acc)


def _l2_rows(x):
    return x * lax.rsqrt(jnp.sum(x * x, axis=-1, keepdims=True) + L2_EPS)


def _gdn_kernel(alog_ref, dtb_ref, nw_ref, cwq_ref, cwk_ref, cwv_ref, csq_ref, csk_ref, csv_ref,
                xq_ref, xk_ref, xv_ref, z_ref, ba_ref, s0_ref, o_ref, sout_ref,
                lhs_ref, add_ref, o0_ref, dec_ref, *, seq, n_heads, hpb, head_c, main_c, group):
    dk = LANES
    lane = lax.broadcasted_iota(jnp.int32, (1, LANES), 1)
    pick = lambda x, idx: jnp.sum(jnp.where(lane == idx, x, 0.0), axis=-1, keepdims=True)
    heads = [pl.program_id(1) * hpb + hh for hh in range(hpb)]
    lanes_of = [slice(hh * LANES, (hh + 1) * LANES) for hh in range(hpb)]
    neg_a = [-jnp.exp(pick(alog_ref[...], h)) for h in heads]
    dtb = [pick(dtb_ref[...], h) for h in heads]

    def prep_all(slot, start, rows, c):
        n = rows // c

        def window(x_ref, cs_ref, ls):
            if isinstance(start, int) and start == 0:
                return jnp.concatenate([cs_ref[0, :, ls], x_ref[0, 0:rows, ls]], axis=0)
            first = start - SUBLANES if isinstance(start, int) else pl.multiple_of(start - SUBLANES, SUBLANES)
            return x_ref[0, pl.ds(first, rows + SUBLANES), ls]

        conv = lambda x_ref, cs_ref, cw_ref: [
            _conv_silu(window(x_ref, cs_ref, ls), cw_ref[0:GDN_CONV, ls]) for ls in lanes_of]
        q, k, v = conv(xq_ref, csq_ref, cwq_ref), conv(xk_ref, csk_ref, cwk_ref), conv(xv_ref, csv_ref, cwv_ref)
        qn = [_l2_rows(x) * (dk ** -0.5) for x in q]
        kn = [_l2_rows(x) for x in k]
        ba = ba_ref[0, pl.ds(start, rows), :]
        beta = [_sigmoid(pick(ba, h)) for h in heads]
        g = [na * _softplus(pick(ba, h + n_heads) + db) for h, na, db in zip(heads, neg_a, dtb)]
        ri = lax.broadcasted_iota(jnp.int32, (rows, rows), 0)
        ci = lax.broadcasted_iota(jnp.int32, (rows, rows), 1)
        causal, strict, upper = ci <= ri, ci < ri, ri <= ci
        if n > 1:
            same = _div_pow2(ri, c) == _div_pow2(ci, c)
            causal, strict, upper = (jnp.logical_and(same, m) for m in (causal, strict, upper))
        tri = jnp.where(causal, 1.0, 0.0)
        if rows % LANES == 0 and 3 * hpb <= LANES:
            part_of_lane = [functools.reduce(jnp.logical_or, [lane == 3 * hh + p for hh in range(hpb)])
                            for p in range(3)]
            packed = jnp.zeros((rows, LANES), F32)
            for hh in range(hpb):
                packed = jnp.where(jnp.logical_and(lane >= 3 * hh, lane < 3 * hh + 3), g[hh], packed)
            hi = packed.astype(BF16).astype(F32)
            mid = (packed - hi).astype(BF16).astype(F32)
            lo = packed - hi - mid
            parts = jnp.where(part_of_lane[0], hi, jnp.where(part_of_lane[1], mid, lo)).astype(BF16)
            sums = _dot(tri.astype(BF16), parts)
            gc = [sums[:, 3 * hh:3 * hh + 1] + (sums[:, 3 * hh + 1:3 * hh + 2] + sums[:, 3 * hh + 2:3 * hh + 3])
                  for hh in range(hpb)]
            g_row = [jnp.transpose(jnp.broadcast_to(x, (rows, LANES)))[0:1, :] for x in gc]
        else:
            gc = [_dot(tri, jnp.broadcast_to(x, (rows, LANES)), HI)[:, 0:1] for x in g]
            ones = jnp.ones((rows, rows), F32)
            g_row = [_dot(ones, jnp.where(upper, x, 0.0), HI) for x in g]
        decay = _each(lambda a, b: jnp.exp(jnp.where(causal, a - b, -jnp.inf)), gc, g_row)
        chunk_of_row = _div_pow2(lax.broadcasted_iota(jnp.int32, (rows, 1), 0), c)

        def chunk_total(x):
            tot = x[c - 1:c, :]
            for cc in range(1, n):
                tot = jnp.where(chunk_of_row == cc, x[(cc + 1) * c - 1:(cc + 1) * c, :], tot)
            return tot

        gl = [chunk_total(x) for x in gc]
        kb = _each(lambda a, b: a * b, kn, beta)
        both = _each(lambda a, b, kk: _mm_nt(jnp.concatenate([a, b], axis=0), kk), kb, qn, kn)
        lm = _each(lambda x, dc: jnp.where(strict, x[:rows] * dc, 0.0), both, decay)
        qk = _each(lambda x, dc: x[rows:] * dc, both, decay)
        t_inv = _unit_lower_inverse(lm, ri, ci, c)
        uw = _each(lambda t, vv, bb, kbb, gg: _mm(t, jnp.concatenate([vv * bb, kbb * jnp.exp(gg)], axis=1)),
                   t_inv, v, beta, kb, gc)
        qkuw = _each(_mm, qk, uw)
        qeff = _each(lambda a, gg, x: a * jnp.exp(gg) - x[:, LANES:], qn, gc, qkuw)
        kd = _each(lambda kk, tot, gg: kk * jnp.exp(tot - gg), kn, gl, gc)
        egl = [jnp.exp(x) for x in gl]
        for hh in range(hpb):
            o0_ref[hh, slot, 0:rows, :] = qkuw[hh][:, :LANES]
        for cc in range(n):
            r = slice(cc * c, (cc + 1) * c)
            ab = _each(lambda a, b: _dot_tn(a[r].astype(BF16), b[r].astype(BF16)), kd, uw)
            for hh in range(hpb):
                lhs_ref[hh, slot, cc, 0:LANES, :] = -ab[hh][:, LANES:]
                lhs_ref[hh, slot, cc, LANES:LANES + c, :] = qeff[hh][r]
                add_ref[hh, slot, cc] = ab[hh][:, :LANES]
                dec_ref[hh, slot, cc] = jnp.broadcast_to(egl[hh][cc * c:cc * c + 1, :], (SUBLANES, LANES))

    def finish(slot, start, rows, c, states):
        states = list(states)
        for cc in range(rows // c):
            for hh in range(hpb):
                s, ls = states[hh], lanes_of[hh]
                r = _mm(lhs_ref[hh, slot, cc, 0:LANES + c, :], s)
                o = r[LANES:] + o0_ref[hh, slot, cc * c:(cc + 1) * c, :]
                states[hh] = s * dec_ref[hh, slot, cc, 0:1, :] + (r[:LANES] + add_ref[hh, slot, cc])
                z = z_ref[0, pl.ds(start + cc * c, c), ls]
                o_ref[0, pl.ds(start + cc * c, c), ls] = (
                    _rms_rows(o, nw_ref[...]) * (z * _sigmoid(z))).astype(o_ref.dtype)
        return tuple(states)

    prep_all(0, 0, head_c, head_c)
    states = finish(0, 0, head_c, head_c, tuple(s0_ref[0, hh] for hh in range(hpb)))
    n_groups = (seq - head_c) // group
    if n_groups:
        gstart = lambda gi: pl.multiple_of(head_c + gi * group, math.gcd(head_c, group))
        prep_all(0, head_c, group, main_c)

        def body(gi, states):
            states = finish(_mod_pow2(gi, 2), gstart(gi), group, main_c, states)
            prep_all(_mod_pow2(gi + 1, 2), gstart(gi + 1), group, main_c)
            return states

        states = lax.fori_loop(0, n_groups - 1, body, states)
        last = n_groups - 1
        states = finish(last % 2, head_c + last * group, group, main_c, states)
    for hh in range(hpb):
        sout_ref[0, hh] = states[hh]


def _gdn(h3, offs, conv_w, conv_state, s0, alog, dtb, norm_w, *, n_heads, hpb, head_c, main_c, group, out_dtype):
    bsz, seq, _ = h3.shape
    assert (seq - head_c) % group == 0 and group % main_c == 0 and n_heads % hpb == 0
    assert all(o % hpb == 0 for o in offs[:4])
    has_main = seq > head_c
    rows_max = group if has_main else head_c
    c_max = max(head_c, main_c) if has_main else head_c
    n_cc = max(rows_max // main_c, 1) if has_main else 1
    oq, ok, ov, oz, oba = offs
    wb = hpb * LANES
    kern = functools.partial(_gdn_kernel, seq=seq, n_heads=n_heads, hpb=hpb, head_c=head_c, main_c=main_c,
                             group=group)
    row = lambda: pl.BlockSpec((1, LANES), lambda b, h: (0, 0))
    cwspec = lambda off: pl.BlockSpec((SUBLANES, wb), lambda b, h: (0, off // hpb + h))
    csspec = lambda off: pl.BlockSpec((1, SUBLANES, wb), lambda b, h: (b, 0, off // hpb + h))
    xspec = lambda off: pl.BlockSpec((1, seq, wb), lambda b, h: (b, 0, off // hpb + h),
                                     pipeline_mode=pl.Buffered(1))
    sspec = pl.BlockSpec((1, hpb, LANES, LANES), lambda b, h: (b, h, 0, 0))
    scratch_bytes = hpb * 2 * (n_cc * (2 * LANES + c_max + SUBLANES) + rows_max) * LANES * 4
    est = (4 * seq * wb * 4 + 2 * seq * LANES * 4 + 2 * seq * wb * jnp.dtype(out_dtype).itemsize
           + scratch_bytes + 8 * hpb * rows_max * rows_max * 4)
    return pl.pallas_call(
        kern,
        grid=(bsz, n_heads // hpb),
        in_specs=[row(), row(), row(),
                  cwspec(0), cwspec(n_heads), cwspec(2 * n_heads),
                  csspec(0), csspec(n_heads), csspec(2 * n_heads),
                  xspec(oq), xspec(ok), xspec(ov), xspec(oz),
                  pl.BlockSpec((1, seq, LANES), lambda b, h: (b, 0, oba)),
                  sspec],
        out_specs=[pl.BlockSpec((1, seq, wb), lambda b, h: (b, 0, h)), sspec],
        out_shape=[jax.ShapeDtypeStruct((bsz, seq, n_heads * LANES), out_dtype),
                   jax.ShapeDtypeStruct(s0.shape, F32)],
        scratch_shapes=[
            pltpu.VMEM((hpb, 2, n_cc, LANES + c_max, LANES), F32),
            pltpu.VMEM((hpb, 2, n_cc, LANES, LANES), F32),
            pltpu.VMEM((hpb, 2, rows_max, LANES), F32),
            pltpu.VMEM((hpb, 2, n_cc, SUBLANES, LANES), F32),
        ],
        compiler_params=pltpu.CompilerParams(
            dimension_semantics=("arbitrary", "arbitrary"), vmem_limit_bytes=_vmem_limit(est)),
        name="gdn",
    )(alog, dtb, norm_w, conv_w, conv_w, conv_w, conv_state, conv_state, conv_state,
      h3, h3, h3, h3, h3, s0)


def _merge_kernel(x_ref, oa_ref, ob_ref, ga_ref, gb_ref, wa_ref, wb_ref, wo_ref, lnw_ref, o_ref):
    ya = _dot(oa_ref[...].astype(BF16), wa_ref[...])
    yb = _dot(ob_ref[...].astype(BF16), wb_ref[...])
    mix = _sigmoid(ga_ref[...]) * ya + _sigmoid(gb_ref[...]) * yb
    y = _dot(mix.astype(BF16), wo_ref[...])
    o_ref[...] = x_ref[...] + _rms_rows(y, lnw_ref[...])


def _merge(x, oa, ob, hflat, gate_blk, wa, wb, wo, lnw, tm):
    m, d = x.shape
    wbr = oa.shape[1]
    assert m % tm == 0
    est = 2 * (4 * tm * d * 4 + 2 * tm * wbr * 4 + (2 * wbr * d + d * d) * 2) + 4 * tm * d * 4
    const = lambda shp: pl.BlockSpec(shp, lambda i: (0, 0), pipeline_mode=pl.Buffered(1))
    return pl.pallas_call(
        _merge_kernel,
        grid=(m // tm,),
        in_specs=[
            pl.BlockSpec((tm, d), lambda i: (i, 0)),
            pl.BlockSpec((tm, wbr), lambda i: (i, 0)),
            pl.BlockSpec((tm, wbr), lambda i: (i, 0)),
            pl.BlockSpec((tm, d), lambda i: (i, gate_blk)),
            pl.BlockSpec((tm, d), lambda i: (i, gate_blk + 1)),
            const(wa.shape), const(wb.shape), const(wo.shape), const((1, d)),
        ],
        out_specs=pl.BlockSpec((tm, d), lambda i: (i, 0)),
        out_shape=jax.ShapeDtypeStruct((m, d), F32),
        compiler_params=pltpu.CompilerParams(
            dimension_semantics=("arbitrary",), vmem_limit_bytes=_vmem_limit(est)),
        name="merge",
    )(x, oa, ob, hflat, hflat, wa, wb, wo, lnw)


def _ffn_kernel(*refs, long_seq, tiles_per_seq, tc):
    if long_seq:
        (x_ref, xprev_ref, st_ref, lnpre_ref, lnpost_ref, win_ref, cw_ref, cb_ref, wd_ref,
         o_ref, tail_ref, xe_ref, act_ref) = refs
    else:
        (x_ref, st1_ref, st2_ref, lnpre_ref, lnpost_ref, win_ref, cw_ref, cb_ref, wd_ref,
         o_ref, tail_ref, xe_ref, act_ref) = refs
    tm = x_ref.shape[0]
    dff = wd_ref.shape[0]
    xn = _rms_rows(x_ref[...], lnpre_ref[...])
    if long_seq:
        xn = jnp.concatenate([_rms_rows(xprev_ref[...], lnpre_ref[...]), xn], axis=0)
        first = (pl.program_id(0) % tiles_per_seq) == 0
    else:
        pos = _mod_pow2(lax.broadcasted_iota(jnp.int32, (tm, 1), 0), SUBLANES)
    xe_ref[...] = xn.astype(BF16)
    for c in range(dff // tc):
        cols = slice(c * tc, (c + 1) * tc)
        xe = xe_ref[...]
        up = _dot(xe, win_ref[:, cols])
        gate = _dot(xe, win_ref[:, dff + c * tc:dff + (c + 1) * tc])
        if long_seq:
            halo = jnp.where(first, st_ref[0, :, cols], gate[0:SUBLANES])
            gate = jnp.concatenate([halo, gate[SUBLANES:]], axis=0)
            g1 = pltpu.roll(gate, 1, 0)[SUBLANES:]
            g2 = pltpu.roll(gate, 2, 0)[SUBLANES:]
            g0 = gate[SUBLANES:]
            up = up[SUBLANES:]
            tail_ref[0, :, cols] = gate[tm:]
        else:
            g1 = jnp.where(pos >= 1, pltpu.roll(gate, 1, 0), st1_ref[:, cols])
            g2 = jnp.where(pos >= 2, pltpu.roll(gate, 2, 0), st2_ref[:, cols])
            g0 = gate
            tail_ref[:, cols] = gate
        y = cw_ref[0:1, cols] * g2 + cw_ref[1:2, cols] * g1 + cw_ref[2:3, cols] * g0 + cb_ref[:, cols]
        act_ref[:, cols] = (_gelu_tanh(y) * up).astype(BF16)
    o_ref[...] = x_ref[...] + _rms_rows(_dot(act_ref[...], wd_ref[...]), lnpost_ref[...])


def _ffn(x, states, lnpre, lnpost, w_in, conv_w, conv_b, w_down, *, tm, tc, long_seq, seq):
    m, d = x.shape
    dff = w_down.shape[0]
    assert m % tm == 0 and dff % tc == 0
    halo = SUBLANES if long_seq else 0
    tiles_per_seq = seq // tm if long_seq else 1
    kern = functools.partial(_ffn_kernel, long_seq=long_seq, tiles_per_seq=tiles_per_seq, tc=tc)
    const = lambda shp: pl.BlockSpec(shp, lambda i: (0,) * len(shp), pipeline_mode=pl.Buffered(1))
    if long_seq:
        assert seq % tm == 0 and tm % SUBLANES == 0
        bpt = tm // SUBLANES
        state_specs = [
            pl.BlockSpec((SUBLANES, d), lambda i: (jnp.maximum(i * bpt - 1, 0), 0)),
            pl.BlockSpec((1, SUBLANES, dff), lambda i: (i // tiles_per_seq, 0, 0)),
        ]
        state_args = [x, states[0]]
        tail_spec = pl.BlockSpec((1, SUBLANES, dff), lambda i: (i, 0, 0))
        tail_shape = jax.ShapeDtypeStruct((m // tm, SUBLANES, dff), F32)
        state_bytes = 2 * SUBLANES * dff * 4
    else:
        state_specs = [pl.BlockSpec((tm, dff), lambda i: (i, 0))] * 2
        state_args = list(states)
        tail_spec = pl.BlockSpec((tm, dff), lambda i: (i, 0))
        tail_shape = jax.ShapeDtypeStruct((m, dff), F32)
        state_bytes = 3 * tm * dff * 4
    est = (2 * (2 * tm * d * 4 + state_bytes) + 3 * d * dff * 2 + (tm + halo) * d * 2 + tm * dff * 2
           + 8 * (tm + halo) * tc * 4 + tm * d * 4)
    return pl.pallas_call(
        kern,
        grid=(m // tm,),
        in_specs=[pl.BlockSpec((tm, d), lambda i: (i, 0))] + state_specs + [
            const((1, d)), const((1, d)), const(w_in.shape), const(conv_w.shape), const(conv_b.shape),
            const(w_down.shape),
        ],
        out_specs=[pl.BlockSpec((tm, d), lambda i: (i, 0)), tail_spec],
        out_shape=[jax.ShapeDtypeStruct((m, d), F32), tail_shape],
        scratch_shapes=[pltpu.VMEM((tm + halo, d), BF16), pltpu.VMEM((tm, dff), BF16)],
        compiler_params=pltpu.CompilerParams(
            dimension_semantics=("arbitrary",), vmem_limit_bytes=_vmem_limit(est)),
        name="ffn",
    )(x, *state_args, lnpre, lnpost, w_in, conv_w, conv_b, w_down)


def _pad_rows_front(a, rows):
    return jnp.pad(a, ((0, 0), (rows - a.shape[1], 0), (0, 0)))


def _pad_rows_back(a, rows):
    return jnp.pad(a, ((0, rows - a.shape[0]), (0, 0)))


def _lane_row(v):
    return jnp.pad(v.astype(F32), (0, LANES - v.shape[0]))[None, :]


def kernel(x_prompt, x_sample, cache_k, cache_v, page_table, state_gdn, state_gdn_conv, state_ffn_conv, meta_tokens, ln_mix_pre, ln_mix_post, ln_ffn_pre, ln_ffn_post, w_in, da_lambda_q1, da_lambda_k1, da_lambda_q2, da_lambda_k2, da_subln, gdn_conv_w, gdn_a_log, gdn_dt_bias, gdn_norm, w_branch_a, w_branch_b, w_out, w_ffn_in, ffn_conv_w, ffn_conv_b, w_down):
    bp, seq, d = x_prompt.shape
    ds, dseq, _ = x_sample.shape
    depth = w_in.shape[0]
    da_heads, da_hd = cache_k.shape[3], cache_k.shape[5]
    gdn_heads, gdn_dk, gdn_dv = state_gdn.shape[2:]
    assert da_heads == gdn_heads and 2 * da_hd == LANES and gdn_dk == LANES and gdn_dv == LANES
    nh = da_heads
    wq = nh * LANES
    dff = w_down.shape[1]
    n_meta = meta_tokens.shape[0]
    tp = n_meta + seq
    assert dseq == SUBLANES

    widths = (wq, wq, wq, 3 * wq, wq, nh, nh, d, d)
    cuts = [0]
    for wdt in widths:
        cuts.append(cuts[-1] + wdt)
    seg = lambda w, k: w[..., cuts[k]:cuts[k + 1]]
    gate_col = 6 * wq
    assert gate_col % d == 0
    offs_gdn = (3 * nh, 4 * nh, 5 * nh, (6 * wq + 2 * d) // LANES, (7 * wq + 2 * d) // LANES)
    w_in_r = jnp.concatenate(
        [seg(w_in, 0), seg(w_in, 1), seg(w_in, 2), seg(w_in, 3), seg(w_in, 7), seg(w_in, 8), seg(w_in, 4),
         seg(w_in, 5), seg(w_in, 6), jnp.zeros((depth, d, LANES - 2 * nh), w_in.dtype)], axis=-1).astype(BF16)
    n_in = w_in_r.shape[-1]
    tn = 1920
    assert n_in % tn == 0

    wa_b, wb_b, wo_b = w_branch_a.astype(BF16), w_branch_b.astype(BF16), w_out.astype(BF16)
    wffn_b, wdown_b = w_ffn_in.astype(BF16), w_down.astype(BF16)
    gconv_w = jnp.pad(gdn_conv_w, ((0, 0), (0, SUBLANES - GDN_CONV), (0, 0)))
    fconv_w = jnp.pad(ffn_conv_w, ((0, 0), (0, SUBLANES - FFN_CONV), (0, 0)))
    lqk = jnp.stack([da_lambda_q1, da_lambda_k1, da_lambda_q2, da_lambda_k2], axis=1).astype(F32)

    xp = jnp.concatenate([jnp.broadcast_to(meta_tokens[None], (bp, n_meta, d)), x_prompt], axis=1).reshape(bp * tp, d)
    xs = x_sample.reshape(ds * dseq, d)
    page = cache_k.shape[2]
    ck = jnp.transpose(cache_k, (0, 1, 3, 4, 5, 2)).reshape(cache_k.shape[:2] + (wq, page))
    cv = cache_v.reshape(cache_v.shape[:2] + (page * nh, 2 * da_hd))
    zero_gconv = jnp.zeros((bp, SUBLANES, 3 * wq), F32)
    zero_s = jnp.zeros((bp, nh, gdn_dk, gdn_dv), F32)
    zero_fconv = jnp.zeros((bp, SUBLANES, dff), F32)
    tm_p = tp // 2
    tm_ffn = tp // 3
    assert tp % 3 == 0 and tm_ffn % SUBLANES == 0
    tm_merge = tp // 3

    outs = [[] for _ in range(10)]
    for l in range(depth):
        lam_init = 0.8 - 0.6 * math.exp(-0.3 * l)
        row = lambda v: v[l][None, :]
        alog, dtb = _lane_row(gdn_a_log[l]), _lane_row(gdn_dt_bias[l])

        def mixer(x, h3, oa, s_conv, s0, head_c, hpb, out_dtype, tm):
            bsz, t, _ = h3.shape
            ob, s_new = _gdn(h3, offs_gdn, gconv_w[l], s_conv, s0, alog, dtb, row(gdn_norm), n_heads=nh, hpb=hpb,
                             head_c=head_c, main_c=GDN_CHUNK, group=GDN_GROUP, out_dtype=out_dtype)
            hflat = h3.reshape(bsz * t, n_in)
            x = _merge(x, oa.reshape(bsz * t, wq), ob.reshape(bsz * t, wq), hflat, gate_col // d,
                       wa_b[l], wb_b[l], wo_b[l], row(ln_mix_post), tm)
            c_new = h3[:, t - (GDN_CONV - 1):, 3 * wq:6 * wq]
            return x, c_new, s_new

        def in_proj(x, bsz, t, tm):
            h, k_rows, v_rows = _norm_matmul(x, row(ln_mix_pre), w_in_r[l], tm, tn, kcol=wq, n_heads=nh)
            return (h.reshape(bsz, t, n_in), k_rows.reshape(bsz, t, nh, 2, da_hd),
                    v_rows.reshape(bsz, t, nh, 2 * da_hd))

        hp3, k_new, v_new = in_proj(xp, bp, tp, tm_p)
        oa = _attn_prompt(hp3, lqk[l], row(da_subln), n_heads=nh, lam_init=lam_init, tq=512)
        xp, c_new, s_new = mixer(xp, hp3, oa, zero_gconv, zero_s, n_meta, nh, BF16, tm_merge)
        xp, tail = _ffn(xp, [zero_fconv], row(ln_ffn_pre), row(ln_ffn_post), wffn_b[l], fconv_w[l],
                        row(ffn_conv_b), wdown_b[l], tm=tm_ffn, tc=256, long_seq=True, seq=tp)
        f_new = tail.reshape(bp, tp // tm_ffn, SUBLANES, dff)[:, -1, SUBLANES - (FFN_CONV - 1):]
        for lst, val in zip(outs[0::2], (k_new, v_new, s_new, c_new, f_new)):
            lst.append(val)

        hs3, k_new, v_new = in_proj(xs, ds, dseq, ds * dseq)
        oa = _attn_sample(hs3, ck, cv, page_table, l, lqk[l], row(da_subln), n_heads=nh, lam_init=lam_init, pps=32)
        xs, c_new, s_new = mixer(
            xs, hs3, oa, _pad_rows_front(state_gdn_conv[l], SUBLANES), state_gdn[l], dseq, nh, F32, ds * dseq)
        buf = state_ffn_conv[l]
        st1 = jnp.pad(buf[:, 1:2], ((0, 0), (0, dseq - 1), (0, 0))).reshape(ds * dseq, dff)
        st2 = jnp.pad(buf, ((0, 0), (0, dseq - 2), (0, 0))).reshape(ds * dseq, dff)
        xs, gate_rows = _ffn(xs, [st1, st2], row(ln_ffn_pre), row(ln_ffn_post), wffn_b[l], fconv_w[l],
                             row(ffn_conv_b), wdown_b[l], tm=ds * dseq, tc=256, long_seq=False, seq=dseq)
        f_new = gate_rows.reshape(ds, dseq, dff)[:, dseq - (FFN_CONV - 1):]
        for lst, val in zip(outs[1::2], (k_new, v_new, s_new, c_new, f_new)):
            lst.append(val)

    kp_l, ks_l, vp_l, vs_l, sp_l, ss_l, cp_l, cs_l, fp_l, fs_l = outs
    return (xp.reshape(bp, tp, d)[:, n_meta:], xs.reshape(ds, dseq, d),
            jnp.stack(kp_l), jnp.stack(vp_l), jnp.stack(ks_l), jnp.stack(vs_l),
            jnp.stack(sp_l), jnp.stack(ss_l), jnp.stack(cp_l), jnp.stack(cs_l),
            jnp.stack(fp_l), jnp.stack(fs_l))
```

```python
import functools
import math

import jax
import jax.numpy as jnp
from jax import lax
from jax.experimental import pallas as pl
from jax.experimental.pallas import tpu as pltpu

F32 = jnp.float32
BF16 = jnp.bfloat16
HI = lax.Precision.HIGHEST

RMS_EPS = 1e-6
L2_EPS = 1e-6
N_META = 16
GDN_CHUNK = 64
GDN_GROUP = 256
GDN_CONV = 4
FFN_CONV = 3
INV_BLOCK = 16

LANES = 128
SUBLANES = 8
VMEM_CAP = 56 << 20


def _vmem_limit(nbytes):
    return int(min(VMEM_CAP, nbytes + (16 << 20)))


def _rms_rows(x, w):
    ms = jnp.mean(x * x, axis=-1, keepdims=True)
    return x * lax.rsqrt(ms + RMS_EPS) * w


def _dot(a, b, precision=None):
    return jnp.dot(a, b, preferred_element_type=F32, precision=precision)


def _dot_nt(a, b, precision=None):
    return lax.dot_general(a, b, (((1,), (1,)), ((), ())), preferred_element_type=F32, precision=precision)


def _dot_tn(a, b, precision=None):
    return lax.dot_general(a, b, (((0,), (0,)), ((), ())), preferred_element_type=F32, precision=precision)


def _div_pow2(x, n):
    assert n & (n - 1) == 0
    if isinstance(x, int):
        return x // n
    return lax.shift_right_logical(x, jnp.asarray(n.bit_length() - 1, x.dtype))


def _mod_pow2(x, n):
    assert n & (n - 1) == 0
    if isinstance(x, int):
        return x % n
    return lax.bitwise_and(x, jnp.asarray(n - 1, x.dtype))


def _sigmoid(x):
    return 1.0 / (1.0 + jnp.exp(-x))


def _softplus(x):
    return jnp.maximum(x, 0.0) + jnp.log1p(jnp.exp(-jnp.abs(x)))


def _gelu_tanh(x):
    c = 0.7978845608028654
    return 0.5 * x * (1.0 + jnp.tanh(c * (x + 0.044715 * (x * x * x))))


def _norm_matmul_kernel(x_ref, lnw_ref, w_ref, o_ref, k_ref, v_ref, xn_ref, *, kcol, n_heads):
    @pl.when(pl.program_id(1) == 0)
    def _():
        xn_ref[...] = _rms_rows(x_ref[...], lnw_ref[...]).astype(BF16)

    res = _dot(xn_ref[...], w_ref[...])
    o_ref[...] = res

    @pl.when(pl.program_id(1) == 0)
    def _():
        tm, wk = k_ref.shape
        k_ref[...] = res[:, kcol:kcol + wk]
        for hh in range(n_heads):
            v_ref[pl.ds(hh, tm, stride=n_heads), :] = res[:, kcol + wk + hh * LANES:kcol + wk + (hh + 1) * LANES]


def _norm_matmul(x, lnw, w, layer, tm, tn, *, kcol, n_heads):
    m, d = x.shape
    n = w.shape[2]
    wk = n_heads * LANES
    assert m % tm == 0 and n % tn == 0 and kcol + 2 * wk <= tn
    est = 2 * (tm * d * 4 + d * tn * 2 + tm * tn * 4 + 2 * tm * wk * 4) + tm * d * 2
    return pl.pallas_call(
        functools.partial(_norm_matmul_kernel, kcol=kcol, n_heads=n_heads),
        grid=(m // tm, n // tn),
        in_specs=[
            pl.BlockSpec((tm, d), lambda i, j: (i, 0)),
            pl.BlockSpec((1, d), lambda i, j: (0, 0)),
            pl.BlockSpec((None, d, tn), lambda i, j: (layer, 0, j)),
        ],
        out_specs=[pl.BlockSpec((tm, tn), lambda i, j: (i, j)),
                   pl.BlockSpec((tm, wk), lambda i, j: (i, 0)),
                   pl.BlockSpec((tm * n_heads, LANES), lambda i, j: (i, 0))],
        out_shape=[jax.ShapeDtypeStruct((m, n), F32),
                   jax.ShapeDtypeStruct((m, wk), F32),
                   jax.ShapeDtypeStruct((m * n_heads, LANES), F32)],
        scratch_shapes=[pltpu.VMEM((tm, d), BF16)],
        compiler_params=pltpu.CompilerParams(
            dimension_semantics=("arbitrary", "arbitrary"), vmem_limit_bytes=_vmem_limit(est)),
        name="in_proj",
    )(x, lnw, w)


def _diff_lambda(lqk, lam_init):
    return (jnp.exp(jnp.sum(lqk[0:1] * lqk[1:2], axis=-1, keepdims=True))
            - jnp.exp(jnp.sum(lqk[2:3] * lqk[3:4], axis=-1, keepdims=True)) + lam_init)


def _sub_norm(o, subw, lam_init):
    return _rms_rows(o, subw) * (1.0 - lam_init)


def _attn_prompt_kernel(lqk_ref, subw_ref, q_ref, k_ref, v_ref, o_ref, kb_ref, vb_ref, *, seq, tq, lam_init):
    dh = LANES // 2
    lam = _diff_lambda(lqk_ref[...], lam_init)
    kb_ref[...] = k_ref[0].astype(BF16)
    vb_ref[...] = v_ref[0].astype(BF16)
    lane = lax.broadcasted_iota(jnp.int32, (1, LANES), 1)
    n_main = seq // tq
    tail = seq - n_main * tq

    def lane_sum(p):
        part = p[:, 0:LANES]
        for c in range(1, p.shape[1] // LANES):
            part = part + p[:, c * LANES:(c + 1) * LANES]
        return jnp.sum(part, axis=-1, keepdims=True)

    def q_block(qs, nq, nk_full, ks_last, tk_last, key_lo):
        q = q_ref[0, qs:qs + nq, :] * (dh ** -0.5)
        row = qs + lax.broadcasted_iota(jnp.int32, (nq, 1), 0)
        col = ks_last + lax.broadcasted_iota(jnp.int32, (1, tk_last), 1)
        keep = col <= row
        if key_lo is not None:
            keep = jnp.logical_and(keep, col >= key_lo)
        k_last, v_last = kb_ref[ks_last:ks_last + tk_last, :], vb_ref[ks_last:ks_last + tk_last, :]
        outs = []
        for qm in (jnp.where(lane < dh, q, 0.0).astype(BF16), jnp.where(lane >= dh, q, 0.0).astype(BF16)):
            s_last = jnp.where(keep, _dot_nt(qm, k_last), -jnp.inf)
            m = jnp.max(s_last, axis=-1, keepdims=True)
            if nk_full:
                s_full = _dot_nt(qm, kb_ref[0:nk_full, :])
                m = jnp.maximum(m, jnp.max(s_full, axis=-1, keepdims=True))
            p = jnp.exp(s_last - m)
            l = lane_sum(p)
            acc = _dot(p.astype(BF16), v_last)
            if nk_full:
                p = jnp.exp(s_full - m)
                l = l + lane_sum(p)
                acc = acc + _dot(p.astype(BF16), vb_ref[0:nk_full, :])
            outs.append(acc / l)
        o = outs[0] - lam * outs[1]
        o_ref[0, qs:qs + nq, :] = _sub_norm(o, subw_ref[...], lam_init).astype(o_ref.dtype)

    for qi in range(n_main):
        q_block(qi * tq, tq, qi * tq, qi * tq, tq, None)
    if tail:
        q_block(n_main * tq, tail, n_main * tq, seq - LANES, LANES, n_main * tq)


def _attn_prompt(h3, lqk, subw, *, n_heads, lam_init, tq):
    bsz, seq, _ = h3.shape
    assert seq >= LANES and (seq - (seq // tq) * tq) % 16 == 0
    kern = functools.partial(_attn_prompt_kernel, seq=seq, tq=tq, lam_init=lam_init)
    blk = lambda off: pl.BlockSpec((1, seq, LANES), lambda b, h: (b, 0, off + h))
    est = 2 * (3 * seq * LANES * 4 + seq * LANES * 2) + 2 * seq * LANES * 2 + 4 * tq * seq * 4
    return pl.pallas_call(
        kern,
        grid=(bsz, n_heads),
        in_specs=[
            pl.BlockSpec(lqk.shape, lambda b, h: (0, 0)),
            pl.BlockSpec((1, LANES), lambda b, h: (0, 0)),
            blk(0), blk(n_heads), blk(2 * n_heads),
        ],
        out_specs=pl.BlockSpec((1, seq, LANES), lambda b, h: (b, 0, h)),
        out_shape=jax.ShapeDtypeStruct((bsz, seq, n_heads * LANES), BF16),
        scratch_shapes=[pltpu.VMEM((seq, LANES), BF16), pltpu.VMEM((seq, LANES), BF16)],
        compiler_params=pltpu.CompilerParams(
            dimension_semantics=("arbitrary", "arbitrary"), vmem_limit_bytes=_vmem_limit(est)),
        name="attn_prompt",
    )(lqk, subw, h3, h3, h3)


def _attn_sample_kernel(pt_ref, lqk_ref, subw_ref, q_ref, kn_ref, vn_ref, *rest, pps, n_heads, lam_init):
    del pt_ref
    kp_refs, vp_refs = rest[:pps], rest[pps:2 * pps]
    o_ref, qrows_ref, m_ref, l_ref, acc_ref = rest[2 * pps:]
    dec_seq, width = q_ref.shape[1], q_ref.shape[2]
    dh = LANES // 2
    n_rows = 2 * n_heads * dec_seq
    page = kp_refs[0].shape[1]
    j = pl.program_id(1)

    @pl.when(j == 0)
    def _():
        q = q_ref[0] * (dh ** -0.5)
        qt = jnp.concatenate([q] * (2 * n_heads), axis=0)
        rgrp = _div_pow2(lax.broadcasted_iota(jnp.int32, (n_rows, width), 0), dec_seq)
        cgrp = _div_pow2(lax.broadcasted_iota(jnp.int32, (n_rows, width), 1), dh)
        qrows_ref[...] = jnp.where(rgrp == cgrp, qt, 0.0).astype(BF16)
        m_ref[...] = jnp.full(m_ref.shape, -jnp.inf, F32)
        l_ref[...] = jnp.zeros(l_ref.shape, F32)
        acc_ref[...] = jnp.zeros(acc_ref.shape, F32)

    qr = qrows_ref[...]

    def update(s_list, v_list):
        smax = s_list[0]
        for s in s_list[1:]:
            smax = jnp.maximum(smax, s)
        m_old = m_ref[...]
        m_new = jnp.maximum(m_old, jnp.max(smax, axis=-1, keepdims=True))
        alpha = jnp.exp(m_old - m_new)
        l = alpha * l_ref[...]
        acc = alpha * acc_ref[...]
        for s, v in zip(s_list, v_list):
            p = jnp.exp(s - m_new)
            l = l + p
            acc = acc + _dot(p.astype(BF16), v)
        m_ref[...] = m_new
        l_ref[...] = l
        acc_ref[...] = acc

    def value_page(vp):
        return jnp.concatenate(
            [vp[pl.ds(hh, page, stride=n_heads), :] for hh in range(n_heads)], axis=1).astype(BF16)

    update([_dot(qr, kp[...].astype(BF16)) for kp in kp_refs], [value_page(vp) for vp in vp_refs])

    @pl.when(j == pl.num_programs(1) - 1)
    def _():
        pad = jnp.zeros((page - dec_seq, width), F32)
        kn = jnp.concatenate([kn_ref[0], pad], axis=0).astype(BF16)
        vn = jnp.concatenate([vn_ref[0], pad], axis=0).astype(BF16)
        qpos = _mod_pow2(lax.broadcasted_iota(jnp.int32, (n_rows, 1), 0), dec_seq)
        col = lax.broadcasted_iota(jnp.int32, (1, page), 1)
        s = jnp.where(col <= qpos, _dot_nt(qr, kn), -jnp.inf)
        update([s], [vn])
        lam = _diff_lambda(lqk_ref[...], lam_init)
        inv_l = 1.0 / jnp.sum(l_ref[...], axis=-1, keepdims=True)
        acc = acc_ref[...] * inv_l
        for h in range(n_heads):
            r0 = 2 * h * dec_seq
            cols = slice(h * LANES, (h + 1) * LANES)
            o = acc[r0:r0 + dec_seq, cols] - lam * acc[r0 + dec_seq:r0 + 2 * dec_seq, cols]
            o_ref[0, :, cols] = _sub_norm(o, subw_ref[...], lam_init).astype(o_ref.dtype)


def _attn_sample(h3, cache_kt, cache_v, page_table, layer, lqk, subw, *, n_heads, lam_init, pps):
    bsz, dec_seq, _ = h3.shape
    n_pages = page_table.shape[1]
    width, page = cache_kt.shape[2], cache_kt.shape[3]
    assert n_pages % pps == 0 and width == n_heads * LANES and cache_v.shape[2:] == (page * n_heads, LANES)
    kern = functools.partial(_attn_sample_kernel, pps=pps, n_heads=n_heads, lam_init=lam_init)
    n_rows = 2 * n_heads * dec_seq

    def page_spec(i):
        return pl.BlockSpec((None, None, width, page), lambda b, j, pt: (layer, pt[b, j * pps + i], 0, 0))

    new_spec = lambda off: pl.BlockSpec((1, dec_seq, width), lambda b, j, pt: (b, 0, off))
    est = 2 * (2 * pps * page * width * 4) + 4 * n_rows * width * 4
    grid_spec = pltpu.PrefetchScalarGridSpec(
        num_scalar_prefetch=1,
        grid=(bsz, n_pages // pps),
        in_specs=[
            pl.BlockSpec(lqk.shape, lambda b, j, pt: (0, 0)),
            pl.BlockSpec((1, LANES), lambda b, j, pt: (0, 0)),
            new_spec(0), new_spec(1), new_spec(2),
        ] + [page_spec(i) for i in range(pps)] + [page_spec(i) for i in range(pps)],
        out_specs=pl.BlockSpec((1, dec_seq, width), lambda b, j, pt: (b, 0, 0)),
        scratch_shapes=[
            pltpu.VMEM((n_rows, width), BF16),
            pltpu.VMEM((n_rows, 1), F32),
            pltpu.VMEM((n_rows, LANES), F32),
            pltpu.VMEM((n_rows, width), F32),
        ],
    )
    return pl.pallas_call(
        kern,
        grid_spec=grid_spec,
        out_shape=jax.ShapeDtypeStruct((bsz, dec_seq, width), F32),
        compiler_params=pltpu.CompilerParams(
            dimension_semantics=("arbitrary", "arbitrary"), vmem_limit_bytes=_vmem_limit(est)),
        name="attn_sample",
    )(page_table, lqk, subw, h3, h3, h3, *([cache_kt] * pps), *([cache_v] * pps))


def _mm(a, b):
    return _dot(a.astype(BF16), b.astype(BF16))


def _mm_nt(a, b):
    return _dot_nt(a.astype(BF16), b.astype(BF16))


def _each(f, *streams):
    return [f(*args) for args in zip(*streams)]


def _nilpotent_inverse(lms, order, eye):
    ts = [eye - lm for lm in lms]
    ps = [lm.astype(BF16) for lm in lms]
    k = 2
    while k < order:
        ps = [_dot(p, p).astype(BF16) for p in ps]
        ts = _each(lambda t, p: t + _dot(t.astype(BF16), p), ts, ps)
        k *= 2
    return ts


def _unit_lower_inverse(lms, ri, ci, c):
    eye = jnp.where(ri == ci, 1.0, 0.0)
    if c <= INV_BLOCK:
        return _nilpotent_inverse(lms, c, eye)
    same = _div_pow2(ri, INV_BLOCK) == _div_pow2(ci, INV_BLOCK)
    diags = [jnp.where(same, lm, 0.0) for lm in lms]
    t_diags = _nilpotent_inverse(diags, INV_BLOCK, eye)
    ns = _each(lambda td, lm, dg: _mm(td, lm - dg), t_diags, lms, diags)
    return _each(_mm, _nilpotent_inverse(ns, c // INV_BLOCK, eye), t_diags)


def _conv_silu(win, w):
    taps = w.shape[0]
    acc = win[SUBLANES:] * w[taps - 1:taps]
    for s in range(1, taps):
        acc = acc + pltpu.roll(win, s, 0)[SUBLANES:] * w[taps - 1 - s:taps - s]
    return acc * _sigmoid(acc)


def _l2_rows(x):
    return x * lax.rsqrt(jnp.sum(x * x, axis=-1, keepdims=True) + L2_EPS)


def _gdn_kernel(alog_ref, dtb_ref, nw_ref, cwq_ref, cwk_ref, cwv_ref, csq_ref, csk_ref, csv_ref,
                xq_ref, xk_ref, xv_ref, z_ref, ba_ref, s0_ref, o_ref, sout_ref,
                lhs_ref, add_ref, o0_ref, dec_ref, *, seq, n_heads, hpb, head_c, main_c, group):
    dk = LANES
    lane = lax.broadcasted_iota(jnp.int32, (1, LANES), 1)
    pick = lambda x, idx: jnp.sum(jnp.where(lane == idx, x, 0.0), axis=-1, keepdims=True)
    heads = [pl.program_id(1) * hpb + hh for hh in range(hpb)]
    lanes_of = [slice(hh * LANES, (hh + 1) * LANES) for hh in range(hpb)]
    neg_a = [-jnp.exp(pick(alog_ref[...], h)) for h in heads]
    dtb = [pick(dtb_ref[...], h) for h in heads]

    def prep_all(slot, start, rows, c):
        n = rows // c

        def window(x_ref, cs_ref, ls):
            if isinstance(start, int) and start == 0:
                return jnp.concatenate([cs_ref[0, :, ls], x_ref[0, 0:rows, ls]], axis=0)
            first = start - SUBLANES if isinstance(start, int) else pl.multiple_of(start - SUBLANES, SUBLANES)
            return x_ref[0, pl.ds(first, rows + SUBLANES), ls]

        conv = lambda x_ref, cs_ref, cw_ref: [
            _conv_silu(window(x_ref, cs_ref, ls), cw_ref[0:GDN_CONV, ls]) for ls in lanes_of]
        q, k, v = conv(xq_ref, csq_ref, cwq_ref), conv(xk_ref, csk_ref, cwk_ref), conv(xv_ref, csv_ref, cwv_ref)
        qn = [_l2_rows(x) * (dk ** -0.5) for x in q]
        kn = [_l2_rows(x) for x in k]
        ba = ba_ref[0, pl.ds(start, rows), :]
        beta = [_sigmoid(pick(ba, h)) for h in heads]
        g = [na * _softplus(pick(ba, h + n_heads) + db) for h, na, db in zip(heads, neg_a, dtb)]
        ri = lax.broadcasted_iota(jnp.int32, (rows, rows), 0)
        ci = lax.broadcasted_iota(jnp.int32, (rows, rows), 1)
        causal, strict, upper = ci <= ri, ci < ri, ri <= ci
        if n > 1:
            same = _div_pow2(ri, c) == _div_pow2(ci, c)
            causal, strict, upper = (jnp.logical_and(same, m) for m in (causal, strict, upper))
        tri = jnp.where(causal, 1.0, 0.0)
        if rows % LANES == 0 and 3 * hpb <= LANES:
            part_of_lane = [functools.reduce(jnp.logical_or, [lane == 3 * hh + p for hh in range(hpb)])
                            for p in range(3)]
            packed = jnp.zeros((rows, LANES), F32)
            for hh in range(hpb):
                packed = jnp.where(jnp.logical_and(lane >= 3 * hh, lane < 3 * hh + 3), g[hh], packed)
            hi = packed.astype(BF16).astype(F32)
            mid = (packed - hi).astype(BF16).astype(F32)
            lo = packed - hi - mid
            parts = jnp.where(part_of_lane[0], hi, jnp.where(part_of_lane[1], mid, lo)).astype(BF16)
            sums = _dot(tri.astype(BF16), parts)
            gc = [sums[:, 3 * hh:3 * hh + 1] + (sums[:, 3 * hh + 1:3 * hh + 2] + sums[:, 3 * hh + 2:3 * hh + 3])
                  for hh in range(hpb)]
            g_row = [jnp.transpose(jnp.broadcast_to(x, (rows, LANES)))[0:1, :] for x in gc]
        else:
            gc = [_dot(tri, jnp.broadcast_to(x, (rows, LANES)), HI)[:, 0:1] for x in g]
            ones = jnp.ones((rows, rows), F32)
            g_row = [_dot(ones, jnp.where(upper, x, 0.0), HI) for x in g]
        decay = _each(lambda a, b: jnp.exp(jnp.where(causal, a - b, -jnp.inf)), gc, g_row)
        chunk_of_row = _div_pow2(lax.broadcasted_iota(jnp.int32, (rows, 1), 0), c)

        def chunk_total(x):
            tot = x[c - 1:c, :]
            for cc in range(1, n):
                tot = jnp.where(chunk_of_row == cc, x[(cc + 1) * c - 1:(cc + 1) * c, :], tot)
            return tot

        gl = [chunk_total(x) for x in gc]
        kb = _each(lambda a, b: a * b, kn, beta)
        both = _each(lambda a, b, kk: _mm_nt(jnp.concatenate([a, b], axis=0), kk), kb, qn, kn)
        lm = _each(lambda x, dc: jnp.where(strict, x[:rows] * dc, 0.0), both, decay)
        qk = _each(lambda x, dc: x[rows:] * dc, both, decay)
        t_inv = _unit_lower_inverse(lm, ri, ci, c)
        uw = _each(lambda t, vv, bb, kbb, gg: _mm(t, jnp.concatenate([vv * bb, kbb * jnp.exp(gg)], axis=1)),
                   t_inv, v, beta, kb, gc)
        qkuw = _each(_mm, qk, uw)
        qeff = _each(lambda a, gg, x: a * jnp.exp(gg) - x[:, LANES:], qn, gc, qkuw)
        kd = _each(lambda kk, tot, gg: kk * jnp.exp(tot - gg), kn, gl, gc)
        egl = [jnp.exp(x) for x in gl]
        for hh in range(hpb):
            o0_ref[hh, slot, 0:rows, :] = qkuw[hh][:, :LANES]
        for cc in range(n):
            r = slice(cc * c, (cc + 1) * c)
            ab = _each(lambda a, b: _dot_tn(a[r].astype(BF16), b[r].astype(BF16)), kd, uw)
            for hh in range(hpb):
                lhs_ref[hh, slot, cc, 0:LANES, :] = -ab[hh][:, LANES:]
                lhs_ref[hh, slot, cc, LANES:LANES + c, :] = qeff[hh][r]
                add_ref[hh, slot, cc] = ab[hh][:, :LANES]
                dec_ref[hh, slot, cc] = jnp.broadcast_to(egl[hh][cc * c:cc * c + 1, :], (SUBLANES, LANES))

    def finish(slot, start, rows, c, states):
        states = list(states)
        for cc in range(rows // c):
            for hh in range(hpb):
                s, ls = states[hh], lanes_of[hh]
                r = _mm(lhs_ref[hh, slot, cc, 0:LANES + c, :], s)
                o = r[LANES:] + o0_ref[hh, slot, cc * c:(cc + 1) * c, :]
                states[hh] = s * dec_ref[hh, slot, cc, 0:1, :] + (r[:LANES] + add_ref[hh, slot, cc])
                z = z_ref[0, pl.ds(start + cc * c, c), ls]
                o_ref[0, pl.ds(start + cc * c, c), ls] = (
                    _rms_rows(o, nw_ref[...]) * (z * _sigmoid(z))).astype(o_ref.dtype)
        return tuple(states)

    prep_all(0, 0, head_c, head_c)
    states = finish(0, 0, head_c, head_c, tuple(s0_ref[0, hh] for hh in range(hpb)))
    n_groups = (seq - head_c) // group
    if n_groups:
        gstart = lambda gi: pl.multiple_of(head_c + gi * group, math.gcd(head_c, group))
        prep_all(0, head_c, group, main_c)

        def body(gi, states):
            states = finish(_mod_pow2(gi, 2), gstart(gi), group, main_c, states)
            prep_all(_mod_pow2(gi + 1, 2), gstart(gi + 1), group, main_c)
            return states

        states = lax.fori_loop(0, n_groups - 1, body, states)
        last = n_groups - 1
        states = finish(last % 2, head_c + last * group, group, main_c, states)
    for hh in range(hpb):
        sout_ref[0, hh] = states[hh]


def _gdn(h3, offs, conv_w, conv_state, s0, s0_layer, alog, dtb, norm_w, *, n_heads, hpb, head_c, main_c, group,
         out_dtype):
    bsz, seq, _ = h3.shape
    assert (seq - head_c) % group == 0 and group % main_c == 0 and n_heads % hpb == 0
    assert all(o % hpb == 0 for o in offs[:4])
    has_main = seq > head_c
    rows_max = group if has_main else head_c
    c_max = max(head_c, main_c) if has_main else head_c
    n_cc = max(rows_max // main_c, 1) if has_main else 1
    oq, ok, ov, oz, oba = offs
    wb = hpb * LANES
    kern = functools.partial(_gdn_kernel, seq=seq, n_heads=n_heads, hpb=hpb, head_c=head_c, main_c=main_c,
                             group=group)
    row = lambda: pl.BlockSpec((1, LANES), lambda b, h: (0, 0))
    cwspec = lambda off: pl.BlockSpec((SUBLANES, wb), lambda b, h: (0, off // hpb + h))
    csspec = lambda off: pl.BlockSpec((1, SUBLANES, wb), lambda b, h: (b, 0, off // hpb + h))
    xspec = lambda off: pl.BlockSpec((1, seq, wb), lambda b, h: (b, 0, off // hpb + h),
                                     pipeline_mode=pl.Buffered(1))
    sspec = pl.BlockSpec((1, hpb, LANES, LANES), lambda b, h: (b, h, 0, 0))
    scratch_bytes = hpb * 2 * (n_cc * (2 * LANES + c_max + SUBLANES) + rows_max) * LANES * 4
    est = (4 * seq * wb * 4 + 2 * seq * LANES * 4 + 2 * seq * wb * jnp.dtype(out_dtype).itemsize
           + scratch_bytes + 8 * hpb * rows_max * rows_max * 4)
    return pl.pallas_call(
        kern,
        grid=(bsz, n_heads // hpb),
        in_specs=[row(), row(), row(),
                  cwspec(0), cwspec(n_heads), cwspec(2 * n_heads),
                  csspec(0), csspec(n_heads), csspec(2 * n_heads),
                  xspec(oq), xspec(ok), xspec(ov), xspec(oz),
                  pl.BlockSpec((1, seq, LANES), lambda b, h: (b, 0, oba)),
                  pl.BlockSpec((None, 1, hpb, LANES, LANES), lambda b, h: (s0_layer, b, h, 0, 0))],
        out_specs=[pl.BlockSpec((1, seq, wb), lambda b, h: (b, 0, h)), sspec],
        out_shape=[jax.ShapeDtypeStruct((bsz, seq, n_heads * LANES), out_dtype),
                   jax.ShapeDtypeStruct(s0.shape[1:], F32)],
        scratch_shapes=[
            pltpu.VMEM((hpb, 2, n_cc, LANES + c_max, LANES), F32),
            pltpu.VMEM((hpb, 2, n_cc, LANES, LANES), F32),
            pltpu.VMEM((hpb, 2, rows_max, LANES), F32),
            pltpu.VMEM((hpb, 2, n_cc, SUBLANES, LANES), F32),
        ],
        compiler_params=pltpu.CompilerParams(
            dimension_semantics=("arbitrary", "arbitrary"), vmem_limit_bytes=_vmem_limit(est)),
        name="gdn",
    )(alog, dtb, norm_w, conv_w, conv_w, conv_w, conv_state, conv_state, conv_state,
      h3, h3, h3, h3, h3, s0)


def _merge_kernel(x_ref, oa_ref, ob_ref, lnpre_ref, wg_ref, wa_ref, wb_ref, wo_ref, lnw_ref, o_ref):
    d = x_ref.shape[1]
    x = x_ref[...]
    gates = _dot(_rms_rows(x, lnpre_ref[...]).astype(BF16), wg_ref[...])
    ya = _dot(oa_ref[...].astype(BF16), wa_ref[...])
    yb = _dot(ob_ref[...].astype(BF16), wb_ref[...])
    mix = _sigmoid(gates[:, :d]) * ya + _sigmoid(gates[:, d:]) * yb
    y = _dot(mix.astype(BF16), wo_ref[...])
    o_ref[...] = x + _rms_rows(y, lnw_ref[...])


def _layer_block(w, layer):
    return pl.BlockSpec((None,) + w.shape[1:], lambda i: (layer, 0, 0), pipeline_mode=pl.Buffered(1))


def _merge(x, oa, ob, lnpre, wg, wa, wb, wo, layer, lnw, tm):
    m, d = x.shape
    wbr = oa.shape[1]
    assert m % tm == 0 and wg.shape[1:] == (d, 2 * d)
    est = 2 * (2 * tm * d * 4 + 2 * tm * wbr * 4) + (2 * wbr * d + 3 * d * d) * 2 + 8 * tm * d * 4
    const = lambda shp: pl.BlockSpec(shp, lambda i: (0, 0), pipeline_mode=pl.Buffered(1))
    return pl.pallas_call(
        _merge_kernel,
        grid=(m // tm,),
        in_specs=[
            pl.BlockSpec((tm, d), lambda i: (i, 0)),
            pl.BlockSpec((tm, wbr), lambda i: (i, 0)),
            pl.BlockSpec((tm, wbr), lambda i: (i, 0)),
            const((1, d)), _layer_block(wg, layer),
            _layer_block(wa, layer), _layer_block(wb, layer), _layer_block(wo, layer), const((1, d)),
        ],
        out_specs=pl.BlockSpec((tm, d), lambda i: (i, 0)),
        out_shape=jax.ShapeDtypeStruct((m, d), F32),
        compiler_params=pltpu.CompilerParams(
            dimension_semantics=("arbitrary",), vmem_limit_bytes=_vmem_limit(est)),
        name="merge",
    )(x, oa, ob, lnpre, wg, wa, wb, wo, lnw)


def _ffn_kernel(*refs, long_seq, tiles_per_seq, tc):
    if long_seq:
        (x_ref, xprev_ref, st_ref, lnpre_ref, lnpost_ref, win_ref, cw_ref, cb_ref, wd_ref,
         o_ref, tail_ref, xe_ref, act_ref) = refs
    else:
        (x_ref, st1_ref, st2_ref, lnpre_ref, lnpost_ref, win_ref, cw_ref, cb_ref, wd_ref,
         o_ref, tail_ref, xe_ref, act_ref) = refs
    tm = x_ref.shape[0]
    dff = wd_ref.shape[0]
    xn = _rms_rows(x_ref[...], lnpre_ref[...])
    if long_seq:
        xn = jnp.concatenate([_rms_rows(xprev_ref[...], lnpre_ref[...]), xn], axis=0)
        first = (pl.program_id(0) % tiles_per_seq) == 0
    else:
        pos = _mod_pow2(lax.broadcasted_iota(jnp.int32, (tm, 1), 0), SUBLANES)
    xe_ref[...] = xn.astype(BF16)
    for c in range(dff // tc):
        cols = slice(c * tc, (c + 1) * tc)
        xe = xe_ref[...]
        up = _dot(xe, win_ref[:, cols])
        gate = _dot(xe, win_ref[:, dff + c * tc:dff + (c + 1) * tc])
        if long_seq:
            halo = jnp.where(first, st_ref[0, :, cols], gate[0:SUBLANES])
            gate = jnp.concatenate([halo, gate[SUBLANES:]], axis=0)
            g1 = pltpu.roll(gate, 1, 0)[SUBLANES:]
            g2 = pltpu.roll(gate, 2, 0)[SUBLANES:]
            g0 = gate[SUBLANES:]
            up = up[SUBLANES:]
            tail_ref[0, :, cols] = gate[tm:]
        else:
            g1 = jnp.where(pos >= 1, pltpu.roll(gate, 1, 0), st1_ref[:, cols])
            g2 = jnp.where(pos >= 2, pltpu.roll(gate, 2, 0), st2_ref[:, cols])
            g0 = gate
            tail_ref[:, cols] = gate
        y = cw_ref[0:1, cols] * g2 + cw_ref[1:2, cols] * g1 + cw_ref[2:3, cols] * g0 + cb_ref[:, cols]
        act_ref[:, cols] = (_gelu_tanh(y) * up).astype(BF16)
    o_ref[...] = x_ref[...] + _rms_rows(_dot(act_ref[...], wd_ref[...]), lnpost_ref[...])


def _ffn(x, states, lnpre, lnpost, w_in, conv_w, conv_b, w_down, layer, *, tm, tc, long_seq, seq):
    m, d = x.shape
    dff = w_down.shape[1]
    assert m % tm == 0 and dff % tc == 0
    halo = SUBLANES if long_seq else 0
    tiles_per_seq = seq // tm if long_seq else 1
    kern = functools.partial(_ffn_kernel, long_seq=long_seq, tiles_per_seq=tiles_per_seq, tc=tc)
    const = lambda shp: pl.BlockSpec(shp, lambda i: (0,) * len(shp), pipeline_mode=pl.Buffered(1))
    if long_seq:
        assert seq % tm == 0 and tm % SUBLANES == 0
        bpt = tm // SUBLANES
        state_specs = [
            pl.BlockSpec((SUBLANES, d), lambda i: (jnp.maximum(i * bpt - 1, 0), 0)),
            pl.BlockSpec((1, SUBLANES, dff), lambda i: (i // tiles_per_seq, 0, 0)),
        ]
        state_args = [x, states[0]]
        tail_spec = pl.BlockSpec((1, SUBLANES, dff), lambda i: (i, 0, 0))
        tail_shape = jax.ShapeDtypeStruct((m // tm, SUBLANES, dff), F32)
        state_bytes = 2 * SUBLANES * dff * 4
    else:
        state_specs = [pl.BlockSpec((tm, dff), lambda i: (i, 0))] * 2
        state_args = list(states)
        tail_spec = pl.BlockSpec((tm, dff), lambda i: (i, 0))
        tail_shape = jax.ShapeDtypeStruct((m, dff), F32)
        state_bytes = 3 * tm * dff * 4
    est = (2 * (2 * tm * d * 4 + state_bytes) + 3 * d * dff * 2 + (tm + halo) * d * 2 + tm * dff * 2
           + 8 * (tm + halo) * tc * 4 + tm * d * 4)
    return pl.pallas_call(
        kern,
        grid=(m // tm,),
        in_specs=[pl.BlockSpec((tm, d), lambda i: (i, 0))] + state_specs + [
            const((1, d)), const((1, d)), _layer_block(w_in, layer), const(conv_w.shape), const(conv_b.shape),
            _layer_block(w_down, layer),
        ],
        out_specs=[pl.BlockSpec((tm, d), lambda i: (i, 0)), tail_spec],
        out_shape=[jax.ShapeDtypeStruct((m, d), F32), tail_shape],
        scratch_shapes=[pltpu.VMEM((tm + halo, d), BF16), pltpu.VMEM((tm, dff), BF16)],
        compiler_params=pltpu.CompilerParams(
            dimension_semantics=("arbitrary",), vmem_limit_bytes=_vmem_limit(est)),
        name="ffn",
    )(x, *state_args, lnpre, lnpost, w_in, conv_w, conv_b, w_down)


def _pad_rows_front(a, rows):
    return jnp.pad(a, ((0, 0), (rows - a.shape[1], 0), (0, 0)))


def _pad_rows_back(a, rows):
    return jnp.pad(a, ((0, rows - a.shape[0]), (0, 0)))


def _lane_row(v):
    return jnp.pad(v.astype(F32), (0, LANES - v.shape[0]))[None, :]


def kernel(x_prompt, x_sample, cache_k, cache_v, page_table, state_gdn, state_gdn_conv, state_ffn_conv, meta_tokens, ln_mix_pre, ln_mix_post, ln_ffn_pre, ln_ffn_post, w_in, da_lambda_q1, da_lambda_k1, da_lambda_q2, da_lambda_k2, da_subln, gdn_conv_w, gdn_a_log, gdn_dt_bias, gdn_norm, w_branch_a, w_branch_b, w_out, w_ffn_in, ffn_conv_w, ffn_conv_b, w_down):
    bp, seq, d = x_prompt.shape
    ds, dseq, _ = x_sample.shape
    depth = w_in.shape[0]
    da_heads, da_hd = cache_k.shape[3], cache_k.shape[5]
    gdn_heads, gdn_dk, gdn_dv = state_gdn.shape[2:]
    assert da_heads == gdn_heads and 2 * da_hd == LANES and gdn_dk == LANES and gdn_dv == LANES
    nh = da_heads
    wq = nh * LANES
    dff = w_down.shape[1]
    n_meta = meta_tokens.shape[0]
    tp = n_meta + seq
    assert dseq == SUBLANES

    n_mix = 7 * wq + 2 * nh
    assert w_in.shape[2] == n_mix + 2 * d
    n_in = -(-n_mix // (2 * LANES)) * 2 * LANES
    offs_gdn = (3 * nh, 4 * nh, 5 * nh, 6 * nh, 7 * nh)
    w_in_r = jnp.pad(w_in[:, :, :n_mix], ((0, 0), (0, 0), (0, n_in - n_mix))).astype(BF16)
    w_gate = w_in[:, :, n_mix:].astype(BF16)
    tn = n_in // 2
    assert tn % LANES == 0 and 3 * wq <= tn

    wa_b, wb_b, wo_b = w_branch_a.astype(BF16), w_branch_b.astype(BF16), w_out.astype(BF16)
    wffn_b, wdown_b = w_ffn_in.astype(BF16), w_down.astype(BF16)
    gconv_w = jnp.pad(gdn_conv_w, ((0, 0), (0, SUBLANES - GDN_CONV), (0, 0)))
    fconv_w = jnp.pad(ffn_conv_w, ((0, 0), (0, SUBLANES - FFN_CONV), (0, 0)))
    lqk = jnp.stack([da_lambda_q1, da_lambda_k1, da_lambda_q2, da_lambda_k2], axis=1).astype(F32)

    xp = jnp.concatenate([jnp.broadcast_to(meta_tokens[None], (bp, n_meta, d)), x_prompt], axis=1).reshape(bp * tp, d)
    xs = x_sample.reshape(ds * dseq, d)
    page = cache_k.shape[2]
    ck = jnp.transpose(cache_k, (0, 1, 3, 4, 5, 2)).reshape(cache_k.shape[:2] + (wq, page))
    cv = cache_v.reshape(cache_v.shape[:2] + (page * nh, 2 * da_hd))
    zero_gconv = jnp.zeros((bp, SUBLANES, 3 * wq), F32)
    zero_s = jnp.zeros((1, bp, nh, gdn_dk, gdn_dv), F32)
    zero_fconv = jnp.zeros((bp, SUBLANES, dff), F32)
    tm_p = tp // 2
    tm_ffn = tp // 3
    assert tp % 3 == 0 and tm_ffn % SUBLANES == 0
    tm_merge = tp // 3

    outs = [[] for _ in range(10)]
    for l in range(depth):
        lam_init = 0.8 - 0.6 * math.exp(-0.3 * l)
        row = lambda v: v[l][None, :]
        alog, dtb = _lane_row(gdn_a_log[l]), _lane_row(gdn_dt_bias[l])

        def mixer(x, h3, oa, s_conv, s0, s0_layer, head_c, hpb, out_dtype, tm):
            bsz, t, _ = h3.shape
            ob, s_new = _gdn(h3, offs_gdn, gconv_w[l], s_conv, s0, s0_layer, alog, dtb, row(gdn_norm), n_heads=nh,
                             hpb=hpb, head_c=head_c, main_c=GDN_CHUNK, group=GDN_GROUP, out_dtype=out_dtype)
            x = _merge(x, oa.reshape(bsz * t, wq), ob.reshape(bsz * t, wq), row(ln_mix_pre), w_gate,
                       wa_b, wb_b, wo_b, l, row(ln_mix_post), tm)
            c_new = h3[:, t - (GDN_CONV - 1):, 3 * wq:6 * wq]
            return x, c_new, s_new

        def in_proj(x, bsz, t, tm):
            h, k_rows, v_rows = _norm_matmul(x, row(ln_mix_pre), w_in_r, l, tm, tn, kcol=wq, n_heads=nh)
            return (h.reshape(bsz, t, n_in), k_rows.reshape(bsz, t, nh, 2, da_hd),
                    v_rows.reshape(bsz, t, nh, 2 * da_hd))

        hp3, k_new, v_new = in_proj(xp, bp, tp, tm_p)
        oa = _attn_prompt(hp3, lqk[l], row(da_subln), n_heads=nh, lam_init=lam_init, tq=512)
        xp, c_new, s_new = mixer(xp, hp3, oa, zero_gconv, zero_s, 0, n_meta, nh, BF16, tm_merge)
        xp, tail = _ffn(xp, [zero_fconv], row(ln_ffn_pre), row(ln_ffn_post), wffn_b, fconv_w[l],
                        row(ffn_conv_b), wdown_b, l, tm=tm_ffn, tc=256, long_seq=True, seq=tp)
        f_new = tail.reshape(bp, tp // tm_ffn, SUBLANES, dff)[:, -1, SUBLANES - (FFN_CONV - 1):]
        for lst, val in zip(outs[0::2], (k_new, v_new, s_new, c_new, f_new)):
            lst.append(val)

        hs3, k_new, v_new = in_proj(xs, ds, dseq, ds * dseq)
        oa = _attn_sample(hs3, ck, cv, page_table, l, lqk[l], row(da_subln), n_heads=nh, lam_init=lam_init, pps=32)
        xs, c_new, s_new = mixer(
            xs, hs3, oa, _pad_rows_front(state_gdn_conv[l], SUBLANES), state_gdn, l, dseq, nh, F32, ds * dseq)
        buf = state_ffn_conv[l]
        st1 = jnp.pad(buf[:, 1:2], ((0, 0), (0, dseq - 1), (0, 0))).reshape(ds * dseq, dff)
        st2 = jnp.pad(buf, ((0, 0), (0, dseq - 2), (0, 0))).reshape(ds * dseq, dff)
        xs, gate_rows = _ffn(xs, [st1, st2], row(ln_ffn_pre), row(ln_ffn_post), wffn_b, fconv_w[l],
                             row(ffn_conv_b), wdown_b, l, tm=ds * dseq, tc=256, long_seq=False, seq=dseq)
        f_new = gate_rows.reshape(ds, dseq, dff)[:, dseq - (FFN_CONV - 1):]
        for lst, val in zip(outs[1::2], (k_new, v_new, s_new, c_new, f_new)):
            lst.append(val)

    kp_l, ks_l, vp_l, vs_l, sp_l, ss_l, cp_l, cs_l, fp_l, fs_l = outs
    return (xp.reshape(bp, tp, d)[:, n_meta:], xs.reshape(ds, dseq, d),
            jnp.stack(kp_l), jnp.stack(vp_l), jnp.stack(ks_l), jnp.stack(vs_l),
            jnp.stack(sp_l), jnp.stack(ss_l), jnp.stack(cp_l), jnp.stack(cs_l),
            jnp.stack(fp_l), jnp.stack(fs_l))
```

```python
import functools
import math

import jax
import jax.numpy as jnp
from jax import lax
from jax.experimental import pallas as pl
from jax.experimental.pallas import tpu as pltpu

F32 = jnp.float32
BF16 = jnp.bfloat16
HI = lax.Precision.HIGHEST

RMS_EPS = 1e-6
L2_EPS = 1e-6
N_META = 16
GDN_CHUNK = 64
GDN_GROUP = 256
GDN_CONV = 4
FFN_CONV = 3
INV_BLOCK = 16

LANES = 128
SUBLANES = 8
VMEM_CAP = 56 << 20


def _vmem_limit(nbytes):
    return int(min(VMEM_CAP, nbytes + (16 << 20)))


def _rms_rows(x, w):
    ms = jnp.mean(x * x, axis=-1, keepdims=True)
    return x * lax.rsqrt(ms + RMS_EPS) * w


def _dot(a, b, precision=None):
    return jnp.dot(a, b, preferred_element_type=F32, precision=precision)


def _dot_nt(a, b, precision=None):
    return lax.dot_general(a, b, (((1,), (1,)), ((), ())), preferred_element_type=F32, precision=precision)


def _dot_tn(a, b, precision=None):
    return lax.dot_general(a, b, (((0,), (0,)), ((), ())), preferred_element_type=F32, precision=precision)


def _div_pow2(x, n):
    assert n & (n - 1) == 0
    if isinstance(x, int):
        return x // n
    return lax.shift_right_logical(x, jnp.asarray(n.bit_length() - 1, x.dtype))


def _mod_pow2(x, n):
    assert n & (n - 1) == 0
    if isinstance(x, int):
        return x % n
    return lax.bitwise_and(x, jnp.asarray(n - 1, x.dtype))


def _sigmoid(x):
    return 1.0 / (1.0 + jnp.exp(-x))


def _softplus(x):
    return jnp.maximum(x, 0.0) + jnp.log1p(jnp.exp(-jnp.abs(x)))


def _gelu_tanh(x):
    c = 0.7978845608028654
    return 0.5 * x * (1.0 + jnp.tanh(c * (x + 0.044715 * (x * x * x))))


def _norm_matmul_kernel(x_ref, lnw_ref, w_ref, o_ref, k_ref, v_ref, xn_ref, *, kcol, n_heads):
    @pl.when(pl.program_id(1) == 0)
    def _():
        xn_ref[...] = _rms_rows(x_ref[...], lnw_ref[...]).astype(BF16)

    res = _dot(xn_ref[...], w_ref[...])
    o_ref[...] = res

    @pl.when(pl.program_id(1) == 0)
    def _():
        tm, wk = k_ref.shape
        k_ref[...] = res[:, kcol:kcol + wk]
        for hh in range(n_heads):
            v_ref[pl.ds(hh, tm, stride=n_heads), :] = res[:, kcol + wk + hh * LANES:kcol + wk + (hh + 1) * LANES]


def _norm_matmul(x, lnw, w, layer, tm, tn, *, kcol, n_heads):
    m, d = x.shape
    n = w.shape[2]
    wk = n_heads * LANES
    assert m % tm == 0 and n % tn == 0 and kcol + 2 * wk <= tn
    est = 2 * (tm * d * 4 + d * tn * 2 + tm * tn * 4 + 2 * tm * wk * 4) + tm * d * 2
    return pl.pallas_call(
        functools.partial(_norm_matmul_kernel, kcol=kcol, n_heads=n_heads),
        grid=(m // tm, n // tn),
        in_specs=[
            pl.BlockSpec((tm, d), lambda i, j: (i, 0)),
            pl.BlockSpec((1, d), lambda i, j: (0, 0)),
            pl.BlockSpec((None, d, tn), lambda i, j: (layer, 0, j)),
        ],
        out_specs=[pl.BlockSpec((tm, tn), lambda i, j: (i, j)),
                   pl.BlockSpec((tm, wk), lambda i, j: (i, 0)),
                   pl.BlockSpec((tm * n_heads, LANES), lambda i, j: (i, 0))],
        out_shape=[jax.ShapeDtypeStruct((m, n), F32),
                   jax.ShapeDtypeStruct((m, wk), F32),
                   jax.ShapeDtypeStruct((m * n_heads, LANES), F32)],
        scratch_shapes=[pltpu.VMEM((tm, d), BF16)],
        compiler_params=pltpu.CompilerParams(
            dimension_semantics=("arbitrary", "arbitrary"), vmem_limit_bytes=_vmem_limit(est)),
        name="in_proj",
    )(x, lnw, w)


def _diff_lambda(lqk, lam_init):
    return (jnp.exp(jnp.sum(lqk[0:1] * lqk[1:2], axis=-1, keepdims=True))
            - jnp.exp(jnp.sum(lqk[2:3] * lqk[3:4], axis=-1, keepdims=True)) + lam_init)


def _sub_norm(o, subw, lam_init):
    return _rms_rows(o, subw) * (1.0 - lam_init)


def _attn_prompt_kernel(lqk_ref, subw_ref, q_ref, k_ref, v_ref, o_ref, kt_ref, kb_ref, vb_ref, *, seq, tq, lam_init):
    dh = LANES // 2
    lam = _diff_lambda(lqk_ref[...], lam_init)
    k = k_ref[0]
    kb_ref[...] = k.astype(BF16)
    vb_ref[...] = v_ref[0].astype(BF16)
    seq_pad = -(-seq // LANES) * LANES
    kt_ref[0] = jnp.transpose(jnp.concatenate([k, jnp.zeros((seq_pad - seq, LANES), F32)], axis=0))[:, :seq]
    lane = lax.broadcasted_iota(jnp.int32, (1, LANES), 1)
    n_main = seq // tq
    tail = seq - n_main * tq

    def lane_sum(p):
        part = p[:, 0:LANES]
        for c in range(1, p.shape[1] // LANES):
            part = part + p[:, c * LANES:(c + 1) * LANES]
        return jnp.sum(part, axis=-1, keepdims=True)

    def q_block(qs, nq, nk_full, ks_last, tk_last, key_lo):
        q = q_ref[0, qs:qs + nq, :] * (dh ** -0.5)
        row = qs + lax.broadcasted_iota(jnp.int32, (nq, 1), 0)
        col = ks_last + lax.broadcasted_iota(jnp.int32, (1, tk_last), 1)
        keep = col <= row
        if key_lo is not None:
            keep = jnp.logical_and(keep, col >= key_lo)
        k_last, v_last = kb_ref[ks_last:ks_last + tk_last, :], vb_ref[ks_last:ks_last + tk_last, :]
        outs = []
        for qm in (jnp.where(lane < dh, q, 0.0).astype(BF16), jnp.where(lane >= dh, q, 0.0).astype(BF16)):
            s_last = jnp.where(keep, _dot_nt(qm, k_last), -jnp.inf)
            m = jnp.max(s_last, axis=-1, keepdims=True)
            if nk_full:
                s_full = _dot_nt(qm, kb_ref[0:nk_full, :])
                m = jnp.maximum(m, jnp.max(s_full, axis=-1, keepdims=True))
            p = jnp.exp(s_last - m)
            l = lane_sum(p)
            acc = _dot(p.astype(BF16), v_last)
            if nk_full:
                p = jnp.exp(s_full - m)
                l = l + lane_sum(p)
                acc = acc + _dot(p.astype(BF16), vb_ref[0:nk_full, :])
            outs.append(acc / l)
        o = outs[0] - lam * outs[1]
        o_ref[0, qs:qs + nq, :] = _sub_norm(o, subw_ref[...], lam_init).astype(o_ref.dtype)

    for qi in range(n_main):
        q_block(qi * tq, tq, qi * tq, qi * tq, tq, None)
    if tail:
        q_block(n_main * tq, tail, n_main * tq, seq - LANES, LANES, n_main * tq)


def _attn_prompt(h3, lqk, subw, *, n_heads, lam_init, tq):
    bsz, seq, _ = h3.shape
    assert seq >= LANES and (seq - (seq // tq) * tq) % 16 == 0
    kern = functools.partial(_attn_prompt_kernel, seq=seq, tq=tq, lam_init=lam_init)
    blk = lambda off: pl.BlockSpec((1, seq, LANES), lambda b, h: (b, 0, off + h))
    est = 2 * (4 * seq * LANES * 4 + seq * LANES * 2) + 2 * seq * LANES * 2 + 4 * tq * seq * 4
    return pl.pallas_call(
        kern,
        grid=(bsz, n_heads),
        in_specs=[
            pl.BlockSpec(lqk.shape, lambda b, h: (0, 0)),
            pl.BlockSpec((1, LANES), lambda b, h: (0, 0)),
            blk(0), blk(n_heads), blk(2 * n_heads),
        ],
        out_specs=[pl.BlockSpec((1, seq, LANES), lambda b, h: (b, 0, h)),
                   pl.BlockSpec((1, LANES, seq), lambda b, h: (b, h, 0))],
        out_shape=[jax.ShapeDtypeStruct((bsz, seq, n_heads * LANES), BF16),
                   jax.ShapeDtypeStruct((bsz, n_heads * LANES, seq), F32)],
        scratch_shapes=[pltpu.VMEM((seq, LANES), BF16), pltpu.VMEM((seq, LANES), BF16)],
        compiler_params=pltpu.CompilerParams(
            dimension_semantics=("arbitrary", "arbitrary"), vmem_limit_bytes=_vmem_limit(est)),
        name="attn_prompt",
    )(lqk, subw, h3, h3, h3)


def _attn_sample_kernel(pt_ref, lqk_ref, subw_ref, q_ref, kn_ref, vn_ref, *rest, pps, n_heads, lam_init):
    del pt_ref
    kp_refs, vp_refs = rest[:pps], rest[pps:2 * pps]
    o_ref, qrows_ref, m_ref, l_ref, acc_ref = rest[2 * pps:]
    dec_seq, width = q_ref.shape[1], q_ref.shape[2]
    dh = LANES // 2
    n_rows = 2 * n_heads * dec_seq
    page = kp_refs[0].shape[1]
    j = pl.program_id(1)

    @pl.when(j == 0)
    def _():
        q = q_ref[0] * (dh ** -0.5)
        qt = jnp.concatenate([q] * (2 * n_heads), axis=0)
        rgrp = _div_pow2(lax.broadcasted_iota(jnp.int32, (n_rows, width), 0), dec_seq)
        cgrp = _div_pow2(lax.broadcasted_iota(jnp.int32, (n_rows, width), 1), dh)
        qrows_ref[...] = jnp.where(rgrp == cgrp, qt, 0.0).astype(BF16)
        m_ref[...] = jnp.full(m_ref.shape, -jnp.inf, F32)
        l_ref[...] = jnp.zeros(l_ref.shape, F32)
        acc_ref[...] = jnp.zeros(acc_ref.shape, F32)

    qr = qrows_ref[...]

    def update(s_list, v_list):
        smax = s_list[0]
        for s in s_list[1:]:
            smax = jnp.maximum(smax, s)
        m_old = m_ref[...]
        m_new = jnp.maximum(m_old, jnp.max(smax, axis=-1, keepdims=True))
        alpha = jnp.exp(m_old - m_new)
        l = alpha * l_ref[...]
        acc = alpha * acc_ref[...]
        for s, v in zip(s_list, v_list):
            p = jnp.exp(s - m_new)
            l = l + p
            acc = acc + _dot(p.astype(BF16), v)
        m_ref[...] = m_new
        l_ref[...] = l
        acc_ref[...] = acc

    def value_page(vp):
        return jnp.concatenate(
            [vp[pl.ds(hh, page, stride=n_heads), :] for hh in range(n_heads)], axis=1).astype(BF16)

    update([_dot(qr, kp[...].astype(BF16)) for kp in kp_refs], [value_page(vp) for vp in vp_refs])

    @pl.when(j == pl.num_programs(1) - 1)
    def _():
        pad = jnp.zeros((page - dec_seq, width), F32)
        kn = jnp.concatenate([kn_ref[0], pad], axis=0).astype(BF16)
        vn = jnp.concatenate([vn_ref[0], pad], axis=0).astype(BF16)
        qpos = _mod_pow2(lax.broadcasted_iota(jnp.int32, (n_rows, 1), 0), dec_seq)
        col = lax.broadcasted_iota(jnp.int32, (1, page), 1)
        s = jnp.where(col <= qpos, _dot_nt(qr, kn), -jnp.inf)
        update([s], [vn])
        lam = _diff_lambda(lqk_ref[...], lam_init)
        inv_l = 1.0 / jnp.sum(l_ref[...], axis=-1, keepdims=True)
        acc = acc_ref[...] * inv_l
        for h in range(n_heads):
            r0 = 2 * h * dec_seq
            cols = slice(h * LANES, (h + 1) * LANES)
            o = acc[r0:r0 + dec_seq, cols] - lam * acc[r0 + dec_seq:r0 + 2 * dec_seq, cols]
            o_ref[0, :, cols] = _sub_norm(o, subw_ref[...], lam_init).astype(o_ref.dtype)


def _attn_sample(h3, cache_kt, cache_v, page_table, layer, lqk, subw, *, n_heads, lam_init, pps):
    bsz, dec_seq, _ = h3.shape
    n_pages = page_table.shape[1]
    width, page = cache_kt.shape[2], cache_kt.shape[3]
    assert n_pages % pps == 0 and width == n_heads * LANES and cache_v.shape[2:] == (page * n_heads, LANES)
    kern = functools.partial(_attn_sample_kernel, pps=pps, n_heads=n_heads, lam_init=lam_init)
    n_rows = 2 * n_heads * dec_seq

    def page_spec(i):
        return pl.BlockSpec((None, None, width, page), lambda b, j, pt: (layer, pt[b, j * pps + i], 0, 0))

    new_spec = lambda off: pl.BlockSpec((1, dec_seq, width), lambda b, j, pt: (b, 0, off))
    est = 2 * (2 * pps * page * width * 4) + 4 * n_rows * width * 4
    grid_spec = pltpu.PrefetchScalarGridSpec(
        num_scalar_prefetch=1,
        grid=(bsz, n_pages // pps),
        in_specs=[
            pl.BlockSpec(lqk.shape, lambda b, j, pt: (0, 0)),
            pl.BlockSpec((1, LANES), lambda b, j, pt: (0, 0)),
            new_spec(0), new_spec(1), new_spec(2),
        ] + [page_spec(i) for i in range(pps)] + [page_spec(i) for i in range(pps)],
        out_specs=pl.BlockSpec((1, dec_seq, width), lambda b, j, pt: (b, 0, 0)),
        scratch_shapes=[
            pltpu.VMEM((n_rows, width), BF16),
            pltpu.VMEM((n_rows, 1), F32),
            pltpu.VMEM((n_rows, LANES), F32),
            pltpu.VMEM((n_rows, width), F32),
        ],
    )
    return pl.pallas_call(
        kern,
        grid_spec=grid_spec,
        out_shape=jax.ShapeDtypeStruct((bsz, dec_seq, width), F32),
        compiler_params=pltpu.CompilerParams(
            dimension_semantics=("arbitrary", "arbitrary"), vmem_limit_bytes=_vmem_limit(est)),
        name="attn_sample",
    )(page_table, lqk, subw, h3, h3, h3, *([cache_kt] * pps), *([cache_v] * pps))


def _mm(a, b):
    return _dot(a.astype(BF16), b.astype(BF16))


def _mm_nt(a, b):
    return _dot_nt(a.astype(BF16), b.astype(BF16))


def _each(f, *streams):
    return [f(*args) for args in zip(*streams)]


def _nilpotent_inverse(lms, order, eye):
    ts = [eye - lm for lm in lms]
    ps = [lm.astype(BF16) for lm in lms]
    k = 2
    while k < order:
        ps = [_dot(p, p).astype(BF16) for p in ps]
        ts = _each(lambda t, p: t + _dot(t.astype(BF16), p), ts, ps)
        k *= 2
    return ts


def _unit_lower_inverse(lms, ri, ci, c):
    eye = jnp.where(ri == ci, 1.0, 0.0)
    if c <= INV_BLOCK:
        return _nilpotent_inverse(lms, c, eye)
    same = _div_pow2(ri, INV_BLOCK) == _div_pow2(ci, INV_BLOCK)
    diags = [jnp.where(same, lm, 0.0) for lm in lms]
    t_diags = _nilpotent_inverse(diags, INV_BLOCK, eye)
    ns = _each(lambda td, lm, dg: _mm(td, lm - dg), t_diags, lms, diags)
    return _each(_mm, _nilpotent_inverse(ns, c // INV_BLOCK, eye), t_diags)


def _conv_silu(win, w):
    taps = w.shape[0]
    acc = win[SUBLANES:] * w[taps - 1:taps]
    for s in range(1, taps):
        acc = acc + pltpu.roll(win, s, 0)[SUBLANES:] * w[taps - 1 - s:taps - s]
    return acc * _sigmoid(acc)


def _l2_rows(x):
    return x * lax.rsqrt(jnp.sum(x * x, axis=-1, keepdims=True) + L2_EPS)


def _gdn_kernel(alog_ref, dtb_ref, nw_ref, cwq_ref, cwk_ref, cwv_ref, csq_ref, csk_ref, csv_ref,
                xq_ref, xk_ref, xv_ref, z_ref, ba_ref, s0_ref, o_ref, sout_ref,
                lhs_ref, add_ref, o0_ref, dec_ref, *, seq, n_heads, hpb, head_c, main_c, group):
    dk = LANES
    lane = lax.broadcasted_iota(jnp.int32, (1, LANES), 1)
    pick = lambda x, idx: jnp.sum(jnp.where(lane == idx, x, 0.0), axis=-1, keepdims=True)
    heads = [pl.program_id(1) * hpb + hh for hh in range(hpb)]
    lanes_of = [slice(hh * LANES, (hh + 1) * LANES) for hh in range(hpb)]
    neg_a = [-jnp.exp(pick(alog_ref[...], h)) for h in heads]
    dtb = [pick(dtb_ref[...], h) for h in heads]

    def prep_all(slot, start, rows, c):
        n = rows // c

        def window(x_ref, cs_ref, ls):
            if isinstance(start, int) and start == 0:
                return jnp.concatenate([cs_ref[0, :, ls], x_ref[0, 0:rows, ls]], axis=0)
            first = start - SUBLANES if isinstance(start, int) else pl.multiple_of(start - SUBLANES, SUBLANES)
            return x_ref[0, pl.ds(first, rows + SUBLANES), ls]

        conv = lambda x_ref, cs_ref, cw_ref: [
            _conv_silu(window(x_ref, cs_ref, ls), cw_ref[0:GDN_CONV, ls]) for ls in lanes_of]
        q, k, v = conv(xq_ref, csq_ref, cwq_ref), conv(xk_ref, csk_ref, cwk_ref), conv(xv_ref, csv_ref, cwv_ref)
        qn = [_l2_rows(x) * (dk ** -0.5) for x in q]
        kn = [_l2_rows(x) for x in k]
        ba = ba_ref[0, pl.ds(start, rows), :]
        beta = [_sigmoid(pick(ba, h)) for h in heads]
        g = [na * _softplus(pick(ba, h + n_heads) + db) for h, na, db in zip(heads, neg_a, dtb)]
        ri = lax.broadcasted_iota(jnp.int32, (rows, rows), 0)
        ci = lax.broadcasted_iota(jnp.int32, (rows, rows), 1)
        causal, strict, upper = ci <= ri, ci < ri, ri <= ci
        if n > 1:
            same = _div_pow2(ri, c) == _div_pow2(ci, c)
            causal, strict, upper = (jnp.logical_and(same, m) for m in (causal, strict, upper))
        tri = jnp.where(causal, 1.0, 0.0)
        if rows % LANES == 0 and 3 * hpb <= LANES:
            part_of_lane = [functools.reduce(jnp.logical_or, [lane == 3 * hh + p for hh in range(hpb)])
                            for p in range(3)]
            packed = jnp.zeros((rows, LANES), F32)
            for hh in range(hpb):
                packed = jnp.where(jnp.logical_and(lane >= 3 * hh, lane < 3 * hh + 3), g[hh], packed)
            hi = packed.astype(BF16).astype(F32)
            mid = (packed - hi).astype(BF16).astype(F32)
            lo = packed - hi - mid
            parts = jnp.where(part_of_lane[0], hi, jnp.where(part_of_lane[1], mid, lo)).astype(BF16)
            sums = _dot(tri.astype(BF16), parts)
            gc = [sums[:, 3 * hh:3 * hh + 1] + (sums[:, 3 * hh + 1:3 * hh + 2] + sums[:, 3 * hh + 2:3 * hh + 3])
                  for hh in range(hpb)]
            g_row = [jnp.transpose(jnp.broadcast_to(x, (rows, LANES)))[0:1, :] for x in gc]
        else:
            gc = [_dot(tri, jnp.broadcast_to(x, (rows, LANES)), HI)[:, 0:1] for x in g]
            ones = jnp.ones((rows, rows), F32)
            g_row = [_dot(ones, jnp.where(upper, x, 0.0), HI) for x in g]
        decay = _each(lambda a, b: jnp.exp(jnp.where(causal, a - b, -jnp.inf)), gc, g_row)
        chunk_of_row = _div_pow2(lax.broadcasted_iota(jnp.int32, (rows, 1), 0), c)

        def chunk_total(x):
            tot = x[c - 1:c, :]
            for cc in range(1, n):
                tot = jnp.where(chunk_of_row == cc, x[(cc + 1) * c - 1:(cc + 1) * c, :], tot)
            return tot

        gl = [chunk_total(x) for x in gc]
        kb = _each(lambda a, b: a * b, kn, beta)
        both = _each(lambda a, b, kk: _mm_nt(jnp.concatenate([a, b], axis=0), kk), kb, qn, kn)
        lm = _each(lambda x, dc: jnp.where(strict, x[:rows] * dc, 0.0), both, decay)
        qk = _each(lambda x, dc: x[rows:] * dc, both, decay)
        t_inv = _unit_lower_inverse(lm, ri, ci, c)
        uw = _each(lambda t, vv, bb, kbb, gg: _mm(t, jnp.concatenate([vv * bb, kbb * jnp.exp(gg)], axis=1)),
                   t_inv, v, beta, kb, gc)
        qkuw = _each(_mm, qk, uw)
        qeff = _each(lambda a, gg, x: a * jnp.exp(gg) - x[:, LANES:], qn, gc, qkuw)
        kd = _each(lambda kk, tot, gg: kk * jnp.exp(tot - gg), kn, gl, gc)
        egl = [jnp.exp(x) for x in gl]
        for hh in range(hpb):
            o0_ref[hh, slot, 0:rows, :] = qkuw[hh][:, :LANES]
        for cc in range(n):
            r = slice(cc * c, (cc + 1) * c)
            ab = _each(lambda a, b: _dot_tn(a[r].astype(BF16), b[r].astype(BF16)), kd, uw)
            for hh in range(hpb):
                lhs_ref[hh, slot, cc, 0:LANES, :] = -ab[hh][:, LANES:]
                lhs_ref[hh, slot, cc, LANES:LANES + c, :] = qeff[hh][r]
                add_ref[hh, slot, cc] = ab[hh][:, :LANES]
                dec_ref[hh, slot, cc] = jnp.broadcast_to(egl[hh][cc * c:cc * c + 1, :], (SUBLANES, LANES))

    def finish(slot, start, rows, c, states):
        states = list(states)
        for cc in range(rows // c):
            for hh in range(hpb):
                s, ls = states[hh], lanes_of[hh]
                r = _mm(lhs_ref[hh, slot, cc, 0:LANES + c, :], s)
                o = r[LANES:] + o0_ref[hh, slot, cc * c:(cc + 1) * c, :]
                states[hh] = s * dec_ref[hh, slot, cc, 0:1, :] + (r[:LANES] + add_ref[hh, slot, cc])
                z = z_ref[0, pl.ds(start + cc * c, c), ls]
                o_ref[0, pl.ds(start + cc * c, c), ls] = (
                    _rms_rows(o, nw_ref[...]) * (z * _sigmoid(z))).astype(o_ref.dtype)
        return tuple(states)

    prep_all(0, 0, head_c, head_c)
    states = finish(0, 0, head_c, head_c, tuple(s0_ref[0, hh] for hh in range(hpb)))
    n_groups = (seq - head_c) // group
    if n_groups:
        gstart = lambda gi: pl.multiple_of(head_c + gi * group, math.gcd(head_c, group))
        prep_all(0, head_c, group, main_c)

        def body(gi, states):
            states = finish(_mod_pow2(gi, 2), gstart(gi), group, main_c, states)
            prep_all(_mod_pow2(gi + 1, 2), gstart(gi + 1), group, main_c)
            return states

        states = lax.fori_loop(0, n_groups - 1, body, states)
        last = n_groups - 1
        states = finish(last % 2, head_c + last * group, group, main_c, states)
    for hh in range(hpb):
        sout_ref[0, hh] = states[hh]


def _gdn(h3, offs, conv_w, conv_state, s0, s0_layer, alog, dtb, norm_w, *, n_heads, hpb, head_c, main_c, group,
         out_dtype):
    bsz, seq, _ = h3.shape
    assert (seq - head_c) % group == 0 and group % main_c == 0 and n_heads % hpb == 0
    assert all(o % hpb == 0 for o in offs[:4])
    has_main = seq > head_c
    rows_max = group if has_main else head_c
    c_max = max(head_c, main_c) if has_main else head_c
    n_cc = max(rows_max // main_c, 1) if has_main else 1
    oq, ok, ov, oz, oba = offs
    wb = hpb * LANES
    kern = functools.partial(_gdn_kernel, seq=seq, n_heads=n_heads, hpb=hpb, head_c=head_c, main_c=main_c,
                             group=group)
    row = lambda: pl.BlockSpec((1, LANES), lambda b, h: (0, 0))
    cwspec = lambda off: pl.BlockSpec((SUBLANES, wb), lambda b, h: (0, off // hpb + h))
    csspec = lambda off: pl.BlockSpec((1, SUBLANES, wb), lambda b, h: (b, 0, off // hpb + h))
    xspec = lambda off: pl.BlockSpec((1, seq, wb), lambda b, h: (b, 0, off // hpb + h),
                                     pipeline_mode=pl.Buffered(1))
    sspec = pl.BlockSpec((1, hpb, LANES, LANES), lambda b, h: (b, h, 0, 0))
    scratch_bytes = hpb * 2 * (n_cc * (2 * LANES + c_max + SUBLANES) + rows_max) * LANES * 4
    est = (4 * seq * wb * 4 + 2 * seq * LANES * 4 + 2 * seq * wb * jnp.dtype(out_dtype).itemsize
           + scratch_bytes + 8 * hpb * rows_max * rows_max * 4)
    return pl.pallas_call(
        kern,
        grid=(bsz, n_heads // hpb),
        in_specs=[row(), row(), row(),
                  cwspec(0), cwspec(n_heads), cwspec(2 * n_heads),
                  csspec(0), csspec(n_heads), csspec(2 * n_heads),
                  xspec(oq), xspec(ok), xspec(ov), xspec(oz),
                  pl.BlockSpec((1, seq, LANES), lambda b, h: (b, 0, oba)),
                  pl.BlockSpec((None, 1, hpb, LANES, LANES), lambda b, h: (s0_layer, b, h, 0, 0))],
        out_specs=[pl.BlockSpec((1, seq, wb), lambda b, h: (b, 0, h)), sspec],
        out_shape=[jax.ShapeDtypeStruct((bsz, seq, n_heads * LANES), out_dtype),
                   jax.ShapeDtypeStruct(s0.shape[1:], F32)],
        scratch_shapes=[
            pltpu.VMEM((hpb, 2, n_cc, LANES + c_max, LANES), F32),
            pltpu.VMEM((hpb, 2, n_cc, LANES, LANES), F32),
            pltpu.VMEM((hpb, 2, rows_max, LANES), F32),
            pltpu.VMEM((hpb, 2, n_cc, SUBLANES, LANES), F32),
        ],
        compiler_params=pltpu.CompilerParams(
            dimension_semantics=("arbitrary", "arbitrary"), vmem_limit_bytes=_vmem_limit(est)),
        name="gdn",
    )(alog, dtb, norm_w, conv_w, conv_w, conv_w, conv_state, conv_state, conv_state,
      h3, h3, h3, h3, h3, s0)


def _merge_kernel(x_ref, oa_ref, ob_ref, lnpre_ref, wg_ref, wa_ref, wb_ref, wo_ref, lnw_ref, o_ref):
    d = x_ref.shape[1]
    x = x_ref[...]
    gates = _dot(_rms_rows(x, lnpre_ref[...]).astype(BF16), wg_ref[...])
    ya = _dot(oa_ref[...].astype(BF16), wa_ref[...])
    yb = _dot(ob_ref[...].astype(BF16), wb_ref[...])
    mix = _sigmoid(gates[:, :d]) * ya + _sigmoid(gates[:, d:]) * yb
    y = _dot(mix.astype(BF16), wo_ref[...])
    o_ref[...] = x + _rms_rows(y, lnw_ref[...])


def _layer_block(w, layer):
    return pl.BlockSpec((None,) + w.shape[1:], lambda i: (layer, 0, 0), pipeline_mode=pl.Buffered(1))


def _merge(x, oa, ob, lnpre, wg, wa, wb, wo, layer, lnw, tm):
    m, d = x.shape
    wbr = oa.shape[1]
    assert m % tm == 0 and wg.shape[1:] == (d, 2 * d)
    est = 2 * (2 * tm * d * 4 + 2 * tm * wbr * 4) + (2 * wbr * d + 3 * d * d) * 2 + 8 * tm * d * 4
    const = lambda shp: pl.BlockSpec(shp, lambda i: (0, 0), pipeline_mode=pl.Buffered(1))
    return pl.pallas_call(
        _merge_kernel,
        grid=(m // tm,),
        in_specs=[
            pl.BlockSpec((tm, d), lambda i: (i, 0)),
            pl.BlockSpec((tm, wbr), lambda i: (i, 0)),
            pl.BlockSpec((tm, wbr), lambda i: (i, 0)),
            const((1, d)), _layer_block(wg, layer),
            _layer_block(wa, layer), _layer_block(wb, layer), _layer_block(wo, layer), const((1, d)),
        ],
        out_specs=pl.BlockSpec((tm, d), lambda i: (i, 0)),
        out_shape=jax.ShapeDtypeStruct((m, d), F32),
        compiler_params=pltpu.CompilerParams(
            dimension_semantics=("arbitrary",), vmem_limit_bytes=_vmem_limit(est)),
        name="merge",
    )(x, oa, ob, lnpre, wg, wa, wb, wo, lnw)


def _ffn_kernel(*refs, long_seq, tiles_per_seq, tc):
    if long_seq:
        (x_ref, xprev_ref, st_ref, lnpre_ref, lnpost_ref, win_ref, cw_ref, cb_ref, wd_ref,
         o_ref, tail_ref, xe_ref, act_ref) = refs
    else:
        (x_ref, st1_ref, st2_ref, lnpre_ref, lnpost_ref, win_ref, cw_ref, cb_ref, wd_ref,
         o_ref, tail_ref, xe_ref, act_ref) = refs
    tm = x_ref.shape[0]
    dff = wd_ref.shape[0]
    xn = _rms_rows(x_ref[...], lnpre_ref[...])
    if long_seq:
        xn = jnp.concatenate([_rms_rows(xprev_ref[...], lnpre_ref[...]), xn], axis=0)
        first = (pl.program_id(0) % tiles_per_seq) == 0
    else:
        pos = _mod_pow2(lax.broadcasted_iota(jnp.int32, (tm, 1), 0), SUBLANES)
    xe_ref[...] = xn.astype(BF16)
    for c in range(dff // tc):
        cols = slice(c * tc, (c + 1) * tc)
        xe = xe_ref[...]
        up = _dot(xe, win_ref[:, cols])
        gate = _dot(xe, win_ref[:, dff + c * tc:dff + (c + 1) * tc])
        if long_seq:
            halo = jnp.where(first, st_ref[0, :, cols], gate[0:SUBLANES])
            gate = jnp.concatenate([halo, gate[SUBLANES:]], axis=0)
            g1 = pltpu.roll(gate, 1, 0)[SUBLANES:]
            g2 = pltpu.roll(gate, 2, 0)[SUBLANES:]
            g0 = gate[SUBLANES:]
            up = up[SUBLANES:]
            tail_ref[0, :, cols] = gate[tm:]
        else:
            g1 = jnp.where(pos >= 1, pltpu.roll(gate, 1, 0), st1_ref[:, cols])
            g2 = jnp.where(pos >= 2, pltpu.roll(gate, 2, 0), st2_ref[:, cols])
            g0 = gate
            tail_ref[:, cols] = gate
        y = cw_ref[0:1, cols] * g2 + cw_ref[1:2, cols] * g1 + cw_ref[2:3, cols] * g0 + cb_ref[:, cols]
        act_ref[:, cols] = (_gelu_tanh(y) * up).astype(BF16)
    o_ref[...] = x_ref[...] + _rms_rows(_dot(act_ref[...], wd_ref[...]), lnpost_ref[...])


def _ffn(x, states, lnpre, lnpost, w_in, conv_w, conv_b, w_down, layer, *, tm, tc, long_seq, seq):
    m, d = x.shape
    dff = w_down.shape[1]
    assert m % tm == 0 and dff % tc == 0
    halo = SUBLANES if long_seq else 0
    tiles_per_seq = seq // tm if long_seq else 1
    kern = functools.partial(_ffn_kernel, long_seq=long_seq, tiles_per_seq=tiles_per_seq, tc=tc)
    const = lambda shp: pl.BlockSpec(shp, lambda i: (0,) * len(shp), pipeline_mode=pl.Buffered(1))
    if long_seq:
        assert seq % tm == 0 and tm % SUBLANES == 0
        bpt = tm // SUBLANES
        state_specs = [
            pl.BlockSpec((SUBLANES, d), lambda i: (jnp.maximum(i * bpt - 1, 0), 0)),
            pl.BlockSpec((1, SUBLANES, dff), lambda i: (i // tiles_per_seq, 0, 0)),
        ]
        state_args = [x, states[0]]
        tail_spec = pl.BlockSpec((1, SUBLANES, dff), lambda i: (i, 0, 0))
        tail_shape = jax.ShapeDtypeStruct((m // tm, SUBLANES, dff), F32)
        state_bytes = 2 * SUBLANES * dff * 4
    else:
        state_specs = [pl.BlockSpec((tm, dff), lambda i: (i, 0))] * 2
        state_args = list(states)
        tail_spec = pl.BlockSpec((tm, dff), lambda i: (i, 0))
        tail_shape = jax.ShapeDtypeStruct((m, dff), F32)
        state_bytes = 3 * tm * dff * 4
    est = (2 * (2 * tm * d * 4 + state_bytes) + 3 * d * dff * 2 + (tm + halo) * d * 2 + tm * dff * 2
           + 8 * (tm + halo) * tc * 4 + tm * d * 4)
    return pl.pallas_call(
        kern,
        grid=(m // tm,),
        in_specs=[pl.BlockSpec((tm, d), lambda i: (i, 0))] + state_specs + [
            const((1, d)), const((1, d)), _layer_block(w_in, layer), const(conv_w.shape), const(conv_b.shape),
            _layer_block(w_down, layer),
        ],
        out_specs=[pl.BlockSpec((tm, d), lambda i: (i, 0)), tail_spec],
        out_shape=[jax.ShapeDtypeStruct((m, d), F32), tail_shape],
        scratch_shapes=[pltpu.VMEM((tm + halo, d), BF16), pltpu.VMEM((tm, dff), BF16)],
        compiler_params=pltpu.CompilerParams(
            dimension_semantics=("arbitrary",), vmem_limit_bytes=_vmem_limit(est)),
        name="ffn",
    )(x, *state_args, lnpre, lnpost, w_in, conv_w, conv_b, w_down)


def _pad_rows_front(a, rows):
    return jnp.pad(a, ((0, 0), (rows - a.shape[1], 0), (0, 0)))


def _pad_rows_back(a, rows):
    return jnp.pad(a, ((0, rows - a.shape[0]), (0, 0)))


def _lane_row(v):
    return jnp.pad(v.astype(F32), (0, LANES - v.shape[0]))[None, :]


def kernel(x_prompt, x_sample, cache_k, cache_v, page_table, state_gdn, state_gdn_conv, state_ffn_conv, meta_tokens, ln_mix_pre, ln_mix_post, ln_ffn_pre, ln_ffn_post, w_in, da_lambda_q1, da_lambda_k1, da_lambda_q2, da_lambda_k2, da_subln, gdn_conv_w, gdn_a_log, gdn_dt_bias, gdn_norm, w_branch_a, w_branch_b, w_out, w_ffn_in, ffn_conv_w, ffn_conv_b, w_down):
    bp, seq, d = x_prompt.shape
    ds, dseq, _ = x_sample.shape
    depth = w_in.shape[0]
    da_heads, da_hd = cache_k.shape[3], cache_k.shape[5]
    gdn_heads, gdn_dk, gdn_dv = state_gdn.shape[2:]
    assert da_heads == gdn_heads and 2 * da_hd == LANES and gdn_dk == LANES and gdn_dv == LANES
    nh = da_heads
    wq = nh * LANES
    dff = w_down.shape[1]
    n_meta = meta_tokens.shape[0]
    tp = n_meta + seq
    assert dseq == SUBLANES

    n_mix = 7 * wq + 2 * nh
    assert w_in.shape[2] == n_mix + 2 * d
    n_in = -(-n_mix // (2 * LANES)) * 2 * LANES
    offs_gdn = (3 * nh, 4 * nh, 5 * nh, 6 * nh, 7 * nh)
    w_in_r = jnp.pad(w_in[:, :, :n_mix], ((0, 0), (0, 0), (0, n_in - n_mix))).astype(BF16)
    w_gate = w_in[:, :, n_mix:].astype(BF16)
    tn = n_in // 2
    assert tn % LANES == 0 and 3 * wq <= tn

    wa_b, wb_b, wo_b = w_branch_a.astype(BF16), w_branch_b.astype(BF16), w_out.astype(BF16)
    wffn_b, wdown_b = w_ffn_in.astype(BF16), w_down.astype(BF16)
    gconv_w = jnp.pad(gdn_conv_w, ((0, 0), (0, SUBLANES - GDN_CONV), (0, 0)))
    fconv_w = jnp.pad(ffn_conv_w, ((0, 0), (0, SUBLANES - FFN_CONV), (0, 0)))
    lqk = jnp.stack([da_lambda_q1, da_lambda_k1, da_lambda_q2, da_lambda_k2], axis=1).astype(F32)

    xp = jnp.concatenate([jnp.broadcast_to(meta_tokens[None], (bp, n_meta, d)), x_prompt], axis=1).reshape(bp * tp, d)
    xs = x_sample.reshape(ds * dseq, d)
    page = cache_k.shape[2]
    ck = jnp.transpose(cache_k, (0, 1, 3, 4, 5, 2)).reshape(cache_k.shape[:2] + (wq, page))
    cv = cache_v.reshape(cache_v.shape[:2] + (page * nh, 2 * da_hd))
    zero_gconv = jnp.zeros((bp, SUBLANES, 3 * wq), F32)
    zero_s = jnp.zeros((1, bp, nh, gdn_dk, gdn_dv), F32)
    zero_fconv = jnp.zeros((bp, SUBLANES, dff), F32)
    tm_p = tp // 2
    tm_ffn = tp // 3
    assert tp % 3 == 0 and tm_ffn % SUBLANES == 0
    tm_merge = tp // 3

    outs = [[] for _ in range(10)]
    for l in range(depth):
        lam_init = 0.8 - 0.6 * math.exp(-0.3 * l)
        row = lambda v: v[l][None, :]
        alog, dtb = _lane_row(gdn_a_log[l]), _lane_row(gdn_dt_bias[l])

        def mixer(x, h3, oa, s_conv, s0, s0_layer, head_c, hpb, out_dtype, tm):
            bsz, t, _ = h3.shape
            ob, s_new = _gdn(h3, offs_gdn, gconv_w[l], s_conv, s0, s0_layer, alog, dtb, row(gdn_norm), n_heads=nh,
                             hpb=hpb, head_c=head_c, main_c=GDN_CHUNK, group=GDN_GROUP, out_dtype=out_dtype)
            x = _merge(x, oa.reshape(bsz * t, wq), ob.reshape(bsz * t, wq), row(ln_mix_pre), w_gate,
                       wa_b, wb_b, wo_b, l, row(ln_mix_post), tm)
            c_new = h3[:, t - (GDN_CONV - 1):, 3 * wq:6 * wq]
            return x, c_new, s_new

        def in_proj(x, bsz, t, tm):
            h, k_rows, v_rows = _norm_matmul(x, row(ln_mix_pre), w_in_r, l, tm, tn, kcol=wq, n_heads=nh)
            return (h.reshape(bsz, t, n_in), k_rows.reshape(bsz, t, nh, 2, da_hd),
                    v_rows.reshape(bsz, t, nh, 2 * da_hd))

        hp3, _, v_new = in_proj(xp, bp, tp, tm_p)
        oa, k_t = _attn_prompt(hp3, lqk[l], row(da_subln), n_heads=nh, lam_init=lam_init, tq=512)
        k_new = jnp.transpose(k_t.reshape(bp, nh, 2, da_hd, tp), (0, 4, 1, 2, 3))
        xp, c_new, s_new = mixer(xp, hp3, oa, zero_gconv, zero_s, 0, n_meta, nh, BF16, tm_merge)
        xp, tail = _ffn(xp, [zero_fconv], row(ln_ffn_pre), row(ln_ffn_post), wffn_b, fconv_w[l],
                        row(ffn_conv_b), wdown_b, l, tm=tm_ffn, tc=256, long_seq=True, seq=tp)
        f_new = tail.reshape(bp, tp // tm_ffn, SUBLANES, dff)[:, -1, SUBLANES - (FFN_CONV - 1):]
        for lst, val in zip(outs[0::2], (k_new, v_new, s_new, c_new, f_new)):
            lst.append(val)

        hs3, k_new, v_new = in_proj(xs, ds, dseq, ds * dseq)
        oa = _attn_sample(hs3, ck, cv, page_table, l, lqk[l], row(da_subln), n_heads=nh, lam_init=lam_init, pps=32)
        xs, c_new, s_new = mixer(
            xs, hs3, oa, _pad_rows_front(state_gdn_conv[l], SUBLANES), state_gdn, l, dseq, nh, F32, ds * dseq)
        buf = state_ffn_conv[l]
        st1 = jnp.pad(buf[:, 1:2], ((0, 0), (0, dseq - 1), (0, 0))).reshape(ds * dseq, dff)
        st2 = jnp.pad(buf, ((0, 0), (0, dseq - 2), (0, 0))).reshape(ds * dseq, dff)
        xs, gate_rows = _ffn(xs, [st1, st2], row(ln_ffn_pre), row(ln_ffn_post), wffn_b, fconv_w[l],
                             row(ffn_conv_b), wdown_b, l, tm=ds * dseq, tc=256, long_seq=False, seq=dseq)
        f_new = gate_rows.reshape(ds, dseq, dff)[:, dseq - (FFN_CONV - 1):]
        for lst, val in zip(outs[1::2], (k_new, v_new, s_new, c_new, f_new)):
            lst.append(val)

    kp_l, ks_l, vp_l, vs_l, sp_l, ss_l, cp_l, cs_l, fp_l, fs_l = outs
    return (xp.reshape(bp, tp, d)[:, n_meta:], xs.reshape(ds, dseq, d),
            jnp.stack(kp_l), jnp.stack(vp_l), jnp.stack(ks_l), jnp.stack(vs_l),
            jnp.stack(sp_l), jnp.stack(ss_l), jnp.stack(cp_l), jnp.stack(cs_l),
            jnp.stack(fp_l), jnp.stack(fs_l))
```
